```python
import math
import jax, jax.numpy as jnp
from jax import lax
import numpy as np

D_MODEL = 1024
BATCH = 8
SEQ = 4096
DEPTH = 4
DEC_BATCH = 16
DEC_SEQ = 4096
PAST_LEN = 128

HEAD_DIM = 64
A_HEADS = 8
A_KV_HEADS = 2
A_GROUP = A_HEADS // A_KV_HEADS
WIN = 128
WIN_BLOCK = 128
B_HEADS = 8
GRID_W = 64
NA_ROWS = 8
NA_COLS = 16
NA_QC = 16
NA_STRIP = 2 * NA_COLS
NA_NCB = GRID_W // NA_QC
C_WIDTH = 512
SSM_GROUP_CH = 16
SSM_GROUPS = C_WIDTH // SSM_GROUP_CH
SSM_STATE = 64
D_WIDTH = 512
CONV_WIDTH = 31
D_FF = 2816
PLE_DIM = 256
N_ATTN_LAYERS = (DEPTH + 1) // 2
N_SSM_LAYERS = DEPTH // 2

A_Q = A_HEADS * HEAD_DIM
A_KV = A_KV_HEADS * HEAD_DIM
B_QKV = B_HEADS * HEAD_DIM
ATTN_IN = A_Q + 2 * A_KV + 3 * B_QKV
ATTN_OUT = A_Q + B_QKV
SSM_IN = C_WIDTH + 2 * D_WIDTH
SSM_OUT = C_WIDTH + D_WIDTH
NEG_INF = -1e30
EPS = 1e-6

kernel_name = 'hybrid_bidir_encoder'


def rms_norm(x, g):
    xf = x.astype(jnp.float32)
    y = xf * lax.rsqrt(jnp.mean(xf * xf, axis=-1, keepdims=True) + EPS)
    return (y * g.astype(jnp.float32)).astype(x.dtype)


def layer_norm(x, g, b):
    xf = x.astype(jnp.float32)
    mu = jnp.mean(xf, axis=-1, keepdims=True)
    xc = xf - mu
    y = xc * lax.rsqrt(jnp.mean(xc * xc, axis=-1, keepdims=True) + EPS)
    return (y * g.astype(jnp.float32) + b.astype(jnp.float32)).astype(x.dtype)


def swiglu(x, w_in, w_out):
    gate, up = jnp.split(x @ w_in, 2, axis=-1)
    return (jax.nn.silu(gate) * up) @ w_out


def alibi_slopes(n):
    return jnp.exp2(-8.0 * jnp.arange(1, n + 1, dtype=jnp.float32) / n)


def window_gqa(q, k, v, q_gain, k_gain, sink):
    b, L = q.shape[0], q.shape[1]
    nb = L // WIN_BLOCK
    qf = rms_norm(q.astype(jnp.float32), q_gain) * (HEAD_DIM ** -0.5)
    kf = rms_norm(k.astype(jnp.float32), k_gain)
    vf = v.astype(jnp.float32)
    pad = ((0, 0), (WIN_BLOCK, WIN_BLOCK), (0, 0), (0, 0))
    kp = jnp.pad(kf, pad).reshape(b, nb + 2, WIN_BLOCK, A_KV_HEADS, HEAD_DIM)
    vp = jnp.pad(vf, pad).reshape(b, nb + 2, WIN_BLOCK, A_KV_HEADS, HEAD_DIM)
    kw = jnp.concatenate([kp[:, :-2], kp[:, 1:-1], kp[:, 2:]], axis=2)
    vw = jnp.concatenate([vp[:, :-2], vp[:, 1:-1], vp[:, 2:]], axis=2)
    qb = qf.reshape(b, nb, WIN_BLOCK, A_KV_HEADS, A_GROUP, HEAD_DIM)
    s = jnp.einsum('bnqhgd,bnkhd->bnhgqk', qb, kw)
    t_pos = jnp.arange(L).reshape(nb, WIN_BLOCK)
    s_pos = (jnp.arange(nb)[:, None] - 1) * WIN_BLOCK + jnp.arange(3 * WIN_BLOCK)[None, :]
    dist = jnp.abs(s_pos[:, None, :] - t_pos[:, :, None])
    valid = (dist <= WIN) & (s_pos[:, None, :] >= 0) & (s_pos[:, None, :] < L)
    slopes = alibi_slopes(A_HEADS).reshape(A_KV_HEADS, A_GROUP)
    bias = -slopes[None, :, :, None, None] * dist[:, None, None].astype(jnp.float32)
    s = jnp.where(valid[:, None, None], s + bias, NEG_INF)
    sk = sink.astype(jnp.float32).reshape(1, 1, A_KV_HEADS, A_GROUP, 1, 1)
    m = jnp.maximum(jnp.max(s, axis=-1, keepdims=True), sk)
    e = jnp.exp(s - m)
    probs = e / (jnp.sum(e, axis=-1, keepdims=True) + jnp.exp(sk - m))
    o = jnp.einsum('bnhgqk,bnkhd->bnqhgd', probs, vw)
    return o.reshape(b, L, A_HEADS * HEAD_DIM)


def neighborhood_attn(q, k, v, q_gain, k_gain, rpb):
    b, L = q.shape[0], q.shape[1]
    rows = L // GRID_W
    kr = min(NA_ROWS, rows)
    qf = rms_norm(q.astype(jnp.float32), q_gain) * (HEAD_DIM ** -0.5)
    kf = rms_norm(k.astype(jnp.float32), k_gain)
    qg = qf.reshape(b, rows, NA_NCB, NA_QC, B_HEADS, HEAD_DIM)
    kg = kf.reshape(b, rows, GRID_W, B_HEADS, HEAD_DIM)
    vg = v.astype(jnp.float32).reshape(b, rows, GRID_W, B_HEADS, HEAD_DIM)
    col_q = jnp.arange(GRID_W).reshape(NA_NCB, NA_QC)
    col_start = jnp.clip(col_q - NA_COLS // 2, 0, GRID_W - NA_COLS)
    strip_start = jnp.clip(jnp.arange(NA_NCB) * NA_QC - NA_COLS // 2, 0, GRID_W - NA_STRIP)
    strip_cols = strip_start[:, None] + jnp.arange(NA_STRIP)[None, :]
    kc = strip_cols[:, None, :]
    col_ok = (kc >= col_start[:, :, None]) & (kc < col_start[:, :, None] + NA_COLS)
    dc_idx = jnp.clip(kc - col_q[:, :, None] + NA_COLS - 1, 0, 2 * NA_COLS - 2)
    rpb_f = rpb.astype(jnp.float32)

    def row_block(r):
        rs = jnp.clip(r - kr // 2, 0, rows - kr)
        k_rows = lax.dynamic_slice_in_dim(kg, rs, kr, axis=1)
        v_rows = lax.dynamic_slice_in_dim(vg, rs, kr, axis=1)
        k_blk = k_rows[:, :, strip_cols]
        v_blk = v_rows[:, :, strip_cols]
        q_r = lax.dynamic_index_in_dim(qg, r, axis=1, keepdims=False)
        s = jnp.einsum('bcqhd,bicjhd->bhcqij', q_r, k_blk)
        dr_idx = rs + jnp.arange(kr) - r + NA_ROWS - 1
        bias = rpb_f[:, dr_idx[None, None, :, None], dc_idx[:, :, None, :]]
        s = jnp.where(col_ok[:, :, None, :], s + bias, NEG_INF)
        p = jax.nn.softmax(s.reshape(s.shape[:4] + (kr * NA_STRIP,)), axis=-1).reshape(s.shape)
        o = jnp.einsum('bhcqij,bicjhd->bcqhd', p, v_blk)
        return o.reshape(b, GRID_W, B_HEADS * HEAD_DIM)

    out = lax.map(row_block, jnp.arange(rows))
    return jnp.moveaxis(out, 0, 1).reshape(b, L, B_HEADS * HEAD_DIM)


def _ssm_combine(left, right):
    a_l, b_l = left
    a_r, b_r = right
    return a_r * a_l, a_r * b_l + b_r


def s5_mixer(u, lam_re, lam_im, log_dt, b_re, b_im, c_re, c_im, d_skip, w_glu, b_glu):
    b, L = u.shape[0], u.shape[1]
    uf = u.astype(jnp.float32).reshape(b, L, SSM_GROUPS, SSM_GROUP_CH)

    def scan_direction(d, seq):
        lam = lax.complex(lam_re[d].astype(jnp.float32), lam_im[d].astype(jnp.float32))
        dt = jnp.exp(log_dt[d].astype(jnp.float32))[:, None]
        lam_bar = jnp.exp(lam * dt)
        b_bar = ((lam_bar - 1.0) / lam)[:, :, None] * lax.complex(b_re[d].astype(jnp.float32), b_im[d].astype(jnp.float32))
        bu = lax.complex(jnp.einsum('blgp,gnp->blgn', seq, jnp.real(b_bar)),
                         jnp.einsum('blgp,gnp->blgn', seq, jnp.imag(b_bar)))
        a = jnp.broadcast_to(lam_bar, (1, L) + lam_bar.shape)
        _, h = lax.associative_scan(_ssm_combine, (a, bu), axis=1)
        return (jnp.einsum('blgn,gpn->blgp', jnp.real(h), c_re[d].astype(jnp.float32))
                - jnp.einsum('blgn,gpn->blgp', jnp.imag(h), c_im[d].astype(jnp.float32)))

    y = (scan_direction(0, uf)
         + jnp.flip(scan_direction(1, jnp.flip(uf, axis=1)), axis=1)
         + d_skip.astype(jnp.float32).reshape(SSM_GROUPS, SSM_GROUP_CH) * uf)
    z = jax.nn.gelu(y.reshape(b, L, C_WIDTH))
    out = z * jax.nn.sigmoid(z @ w_glu.astype(jnp.float32) + b_glu.astype(jnp.float32))
    return out.astype(u.dtype)


def conv_module(a, g, conv_w, conv_b, ln_g, ln_b):
    h = a * jax.nn.sigmoid(g)
    h = lax.conv_general_dilated(h, conv_w[:, None, :], window_strides=(1,),
                                 padding=[(CONV_WIDTH // 2, CONV_WIDTH // 2)],
                                 dimension_numbers=('NWC', 'WIO', 'NWC'),
                                 feature_group_count=D_WIDTH) + conv_b
    return jax.nn.silu(layer_norm(h, ln_g, ln_b))


def attn_mixer(u, w_in, q_gain_a, k_gain_a, sink_a, q_gain_b, k_gain_b, rpb_b, w_out):
    b, L = u.shape[0], u.shape[1]
    h = u @ w_in
    qa, ka, va, qb, kb, vb = jnp.split(
        h, [A_Q, A_Q + A_KV, A_Q + 2 * A_KV, A_Q + 2 * A_KV + B_QKV, A_Q + 2 * A_KV + 2 * B_QKV], axis=-1)
    ya = window_gqa(qa.reshape(b, L, A_HEADS, HEAD_DIM), ka.reshape(b, L, A_KV_HEADS, HEAD_DIM),
                    va.reshape(b, L, A_KV_HEADS, HEAD_DIM), q_gain_a, k_gain_a, sink_a)
    yb = neighborhood_attn(qb.reshape(b, L, B_HEADS, HEAD_DIM), kb.reshape(b, L, B_HEADS, HEAD_DIM),
                           vb.reshape(b, L, B_HEADS, HEAD_DIM), q_gain_b, k_gain_b, rpb_b)
    return jnp.concatenate([ya, yb], axis=-1).astype(u.dtype) @ w_out


def ssm_conv_mixer(u, w_in, lam_re, lam_im, log_dt, b_re, b_im, c_re, c_im, d_skip, w_glu, b_glu,
                   conv_w, conv_b, ln_g, ln_b, w_out):
    h = u @ w_in
    uc, ad, gd = jnp.split(h, [C_WIDTH, C_WIDTH + D_WIDTH], axis=-1)
    yc = s5_mixer(uc, lam_re, lam_im, log_dt, b_re, b_im, c_re, c_im, d_skip, w_glu, b_glu)
    yd = conv_module(ad, gd, conv_w, conv_b, ln_g, ln_b)
    return jnp.concatenate([yc, yd.astype(yc.dtype)], axis=-1) @ w_out


def encoder_trunk(x, p, w):
    for i in range(DEPTH):
        x = x + 0.5 * swiglu(rms_norm(x, w['norm_ffn1'][i]), w['w_ffn1_in'][i], w['w_ffn1_out'][i])
        u = rms_norm(x, w['norm_mix'][i])
        if i % 2 == 0:
            j = i // 2
            x = x + attn_mixer(u, w['w_attn_in'][j], w['q_gain_a'][j], w['k_gain_a'][j], w['sink_a'][j],
                               w['q_gain_b'][j], w['k_gain_b'][j], w['rpb_b'][j], w['w_attn_out'][j])
        else:
            j = i // 2
            x = x + ssm_conv_mixer(u, w['w_ssm_in'][j], w['lam_re'][j], w['lam_im'][j], w['log_dt'][j],
                                   w['b_re'][j], w['b_im'][j], w['c_re'][j], w['c_im'][j], w['d_skip'][j],
                                   w['w_glu_c'][j], w['b_glu_c'][j], w['conv_w'][j], w['conv_b'][j],
                                   w['ln_g_d'][j], w['ln_b_d'][j], w['w_ssm_out'][j])
        x = x + 0.5 * swiglu(rms_norm(x, w['norm_ffn2'][i]), w['w_ffn2_in'][i], w['w_ffn2_out'][i])
        gate = jax.nn.sigmoid(rms_norm(x, w['norm_ple'][i]) @ w['w_ple_gate'][i])
        x = x + gate * rms_norm(p[i] @ w['w_ple_proj'][i], w['norm_ple_post'][i])
    return x


def setup_inputs(seed: int = 0) -> dict:
    key = jax.random.key(seed)
    ks = iter(jax.random.split(key, 64))
    f32 = jnp.float32

    def nrm(shape, scale):
        return jax.random.normal(next(ks), shape, f32) * scale

    def gain(shape):
        return 1.0 + nrm(shape, 0.02)

    NA, NS = N_ATTN_LAYERS, N_SSM_LAYERS
    G, N, P = SSM_GROUPS, SSM_STATE, SSM_GROUP_CH
    return {
        'x_prompt': nrm((BATCH, SEQ, D_MODEL), 1.0),
        'x_sample': nrm((DEC_BATCH, DEC_SEQ, D_MODEL), 1.0),
        'p_prompt': nrm((DEPTH, BATCH, SEQ, PLE_DIM), 1.0),
        'p_sample': nrm((DEPTH, DEC_BATCH, DEC_SEQ, PLE_DIM), 1.0),
        'norm_ffn1': gain((DEPTH, D_MODEL)),
        'w_ffn1_in': nrm((DEPTH, D_MODEL, 2 * D_FF), D_MODEL ** -0.5),
        'w_ffn1_out': nrm((DEPTH, D_FF, D_MODEL), D_FF ** -0.5),
        'norm_mix': gain((DEPTH, D_MODEL)),
        'norm_ffn2': gain((DEPTH, D_MODEL)),
        'w_ffn2_in': nrm((DEPTH, D_MODEL, 2 * D_FF), D_MODEL ** -0.5),
        'w_ffn2_out': nrm((DEPTH, D_FF, D_MODEL), D_FF ** -0.5),
        'norm_ple': gain((DEPTH, D_MODEL)),
        'w_ple_gate': nrm((DEPTH, D_MODEL, D_MODEL), D_MODEL ** -0.5),
        'w_ple_proj': nrm((DEPTH, PLE_DIM, D_MODEL), PLE_DIM ** -0.5),
        'norm_ple_post': gain((DEPTH, D_MODEL)),
        'w_attn_in': nrm((NA, D_MODEL, ATTN_IN), D_MODEL ** -0.5),
        'q_gain_a': gain((NA, HEAD_DIM)),
        'k_gain_a': gain((NA, HEAD_DIM)),
        'sink_a': nrm((NA, A_HEADS), 0.5),
        'q_gain_b': gain((NA, HEAD_DIM)),
        'k_gain_b': gain((NA, HEAD_DIM)),
        'rpb_b': nrm((NA, B_HEADS, 2 * NA_ROWS - 1, 2 * NA_COLS - 1), 0.1),
        'w_attn_out': nrm((NA, ATTN_OUT, D_MODEL), ATTN_OUT ** -0.5),
        'w_ssm_in': nrm((NS, D_MODEL, SSM_IN), D_MODEL ** -0.5),
        'lam_re': -0.5 + nrm((NS, 2, G, N), 0.01),
        'lam_im': jnp.broadcast_to(math.pi * jnp.arange(N, dtype=f32), (NS, 2, G, N)) + nrm((NS, 2, G, N), 0.01),
        'log_dt': jax.random.uniform(next(ks), (NS, 2, G), f32, math.log(1e-3), math.log(1e-1)),
        'b_re': nrm((NS, 2, G, N, P), (2 * P) ** -0.5),
        'b_im': nrm((NS, 2, G, N, P), (2 * P) ** -0.5),
        'c_re': nrm((NS, 2, G, P, N), N ** -0.5),
        'c_im': nrm((NS, 2, G, P, N), N ** -0.5),
        'd_skip': nrm((NS, C_WIDTH), 1.0),
        'w_glu_c': nrm((NS, C_WIDTH, C_WIDTH), C_WIDTH ** -0.5),
        'b_glu_c': nrm((NS, C_WIDTH), 0.02),
        'conv_w': nrm((NS, CONV_WIDTH, D_WIDTH), CONV_WIDTH ** -0.5),
        'conv_b': nrm((NS, D_WIDTH), 0.02),
        'ln_g_d': gain((NS, D_WIDTH)),
        'ln_b_d': nrm((NS, D_WIDTH), 0.02),
        'w_ssm_out': nrm((NS, SSM_OUT, D_MODEL), SSM_OUT ** -0.5),
    }


def reference(x_prompt, x_sample, p_prompt, p_sample,
              norm_ffn1, w_ffn1_in, w_ffn1_out, norm_mix, norm_ffn2, w_ffn2_in, w_ffn2_out,
              norm_ple, w_ple_gate, w_ple_proj, norm_ple_post,
              w_attn_in, q_gain_a, k_gain_a, sink_a, q_gain_b, k_gain_b, rpb_b, w_attn_out,
              w_ssm_in, lam_re, lam_im, log_dt, b_re, b_im, c_re, c_im, d_skip, w_glu_c, b_glu_c,
              conv_w, conv_b, ln_g_d, ln_b_d, w_ssm_out):
    w = dict(norm_ffn1=norm_ffn1, w_ffn1_in=w_ffn1_in, w_ffn1_out=w_ffn1_out, norm_mix=norm_mix,
             norm_ffn2=norm_ffn2, w_ffn2_in=w_ffn2_in, w_ffn2_out=w_ffn2_out,
             norm_ple=norm_ple, w_ple_gate=w_ple_gate, w_ple_proj=w_ple_proj, norm_ple_post=norm_ple_post,
             w_attn_in=w_attn_in, q_gain_a=q_gain_a, k_gain_a=k_gain_a, sink_a=sink_a,
             q_gain_b=q_gain_b, k_gain_b=k_gain_b, rpb_b=rpb_b, w_attn_out=w_attn_out,
             w_ssm_in=w_ssm_in, lam_re=lam_re, lam_im=lam_im, log_dt=log_dt, b_re=b_re, b_im=b_im,
             c_re=c_re, c_im=c_im, d_skip=d_skip, w_glu_c=w_glu_c, b_glu_c=b_glu_c,
             conv_w=conv_w, conv_b=conv_b, ln_g_d=ln_g_d, ln_b_d=ln_b_d, w_ssm_out=w_ssm_out)
    y_prompt = encoder_trunk(x_prompt, p_prompt, w)
    y_sample = encoder_trunk(x_sample, p_sample, w)
    return (y_prompt, y_sample)
```

```python
import functools
import math

import jax
import jax.numpy as jnp
from jax import lax
from jax.experimental import pallas as pl
from jax.experimental.pallas import tpu as pltpu

D_MODEL = 1024
DEPTH = 4
HEAD_DIM = 64
A_HEADS = 8
A_KV_HEADS = 2
A_GROUP = A_HEADS // A_KV_HEADS
WIN = 128
B_HEADS = 8
GRID_W = 64
NA_ROWS = 8
NA_COLS = 16
C_WIDTH = 512
SSM_GROUP_CH = 16
SSM_GROUPS = C_WIDTH // SSM_GROUP_CH
SSM_STATE = 64
D_WIDTH = 512
CONV_WIDTH = 31
D_FF = 2816
PLE_DIM = 256
A_Q = A_HEADS * HEAD_DIM
A_KV = A_KV_HEADS * HEAD_DIM
B_QKV = B_HEADS * HEAD_DIM
ATTN_IN = A_Q + 2 * A_KV + 3 * B_QKV
NEG_INF = -1e30
EPS = 1e-6

BF16 = jnp.bfloat16
F32 = jnp.float32

VMEM_LIMIT_BYTES = 52 * 1024 * 1024
LANES = 128
SUBLANES = 8
TOKEN_TILE = 512
FF_CHUNK = 256
SCAN_CHUNK = 64
SCAN_SEQS = SUBLANES
NA_QROWS = 8
NA_KBLK_ROWS = 4
NA_KBLKS = 4
CONV_HALO = 16
CONV_ROWS = 64


def _params(*sem):
    return pltpu.CompilerParams(dimension_semantics=sem, vmem_limit_bytes=VMEM_LIMIT_BYTES)


def _rms(x, g):
    return x * lax.rsqrt(jnp.mean(x * x, axis=-1, keepdims=True) + EPS) * g


def _dot(a, b):
    return jnp.dot(a, b, preferred_element_type=F32)


def _dot_nt(a, b):
    return lax.dot_general(a, b, (((1,), (1,)), ((), ())), preferred_element_type=F32)


def _const_spec(shape):
    nd = len(shape)
    return pl.BlockSpec(shape, lambda *_: (0,) * nd)


def _ffn_kernel(x_ref, g_ref, win_ref, wout_ref, o_ref, acc_ref):
    x = x_ref[...]
    xn = _rms(x, g_ref[...]).astype(BF16)
    for c in range(D_FF // FF_CHUNK):
        lo = c * FF_CHUNK
        gate = _dot(xn, win_ref[:, lo:lo + FF_CHUNK])
        up = _dot(xn, win_ref[:, D_FF + lo:D_FF + lo + FF_CHUNK])
        act = (gate * jax.nn.sigmoid(gate) * up).astype(BF16)
        part = _dot(act, wout_ref[lo:lo + FF_CHUNK, :])
        if c == 0:
            acc_ref[...] = part
        else:
            acc_ref[...] += part
    o_ref[...] = x + 0.5 * acc_ref[...]


def _ffn(x2d, g, w_in, w_out):
    t = x2d.shape[0]
    return pl.pallas_call(
        _ffn_kernel,
        out_shape=jax.ShapeDtypeStruct(x2d.shape, F32),
        grid=(t // TOKEN_TILE,),
        in_specs=[
            pl.BlockSpec((TOKEN_TILE, D_MODEL), lambda i: (i, 0)),
            _const_spec((1, D_MODEL)),
            _const_spec((D_MODEL, 2 * D_FF)),
            _const_spec((D_FF, D_MODEL)),
        ],
        out_specs=pl.BlockSpec((TOKEN_TILE, D_MODEL), lambda i: (i, 0)),
        scratch_shapes=[pltpu.VMEM((TOKEN_TILE, D_MODEL), F32)],
        compiler_params=_params("parallel"),
        name="ffn",
    )(x2d, g, w_in, w_out)


def _ple_kernel(x_ref, p_ref, g1_ref, wg_ref, wp_ref, g2_ref, o_ref):
    x = x_ref[...]
    gate = jax.nn.sigmoid(_dot(_rms(x, g1_ref[...]).astype(BF16), wg_ref[...]))
    proj = _dot(p_ref[...].astype(BF16), wp_ref[...])
    o_ref[...] = x + gate * _rms(proj, g2_ref[...])


def _ple(x2d, p3d, layer, g1, wg, wp, g2):
    t = x2d.shape[0]
    return pl.pallas_call(
        _ple_kernel,
        out_shape=jax.ShapeDtypeStruct(x2d.shape, F32),
        grid=(t // TOKEN_TILE,),
        in_specs=[
            pl.BlockSpec((TOKEN_TILE, D_MODEL), lambda i: (i, 0)),
            pl.BlockSpec((None, TOKEN_TILE, PLE_DIM), lambda i: (layer, i, 0)),
            _const_spec((1, D_MODEL)),
            _const_spec((D_MODEL, D_MODEL)),
            _const_spec((PLE_DIM, D_MODEL)),
            _const_spec((1, D_MODEL)),
        ],
        out_specs=pl.BlockSpec((TOKEN_TILE, D_MODEL), lambda i: (i, 0)),
        compiler_params=_params("parallel"),
        name="ple",
    )(x2d, p3d, g1, wg, wp, g2)


def _attn_in_kernel(x_ref, g_ref, w_ref, qa_ref, kva_ref, qb_ref, kb_ref, vb_ref):
    h = _dot(_rms(x_ref[...], g_ref[...]).astype(BF16), w_ref[...])
    o = 0
    for ref in (qa_ref, kva_ref, qb_ref, kb_ref, vb_ref):
        width = ref.shape[-1]
        ref[...] = h[:, o:o + width]
        o += width


def _attn_in(x2d, g, w):
    t = x2d.shape[0]
    widths = (A_Q, 2 * A_KV, B_QKV, B_QKV, B_QKV)
    return pl.pallas_call(
        _attn_in_kernel,
        out_shape=[jax.ShapeDtypeStruct((t, n), F32) for n in widths],
        grid=(t // TOKEN_TILE,),
        in_specs=[
            pl.BlockSpec((TOKEN_TILE, D_MODEL), lambda i: (i, 0)),
            _const_spec((1, D_MODEL)),
            _const_spec((D_MODEL, ATTN_IN)),
        ],
        out_specs=[pl.BlockSpec((TOKEN_TILE, n), lambda i: (i, 0)) for n in widths],
        compiler_params=_params("parallel"),
        name="attn_in",
    )(x2d, g, w)


def _head_norm(x, g, scale=1.0):
    return x * (lax.rsqrt(jnp.mean(x * x, axis=-1, keepdims=True) + EPS) * scale) * g


def _win_gqa_kernel(sink_ref, qa_ref, kva_ref, qg_ref, kg_ref, o_ref, *, seq_len):
    n = pl.program_id(1)
    span = 3 * WIN
    start = pl.multiple_of(jnp.clip((n - 1) * WIN, 0, seq_len - span), WIN)
    kv = kva_ref[pl.ds(start, span), :]
    t_pos = n * WIN + lax.broadcasted_iota(jnp.int32, (WIN, span), 0)
    s_pos = start + lax.broadcasted_iota(jnp.int32, (WIN, span), 1)
    dist = jnp.abs(s_pos - t_pos)
    valid = dist <= WIN
    dist_f = dist.astype(F32)
    for hk in range(A_KV_HEADS):
        k = kv[:, hk * HEAD_DIM:(hk + 1) * HEAD_DIM]
        v = kv[:, A_KV + hk * HEAD_DIM:A_KV + (hk + 1) * HEAD_DIM]
        kn = _head_norm(k, kg_ref[...]).astype(BF16)
        vb = v.astype(BF16)
        for g in range(A_GROUP):
            head = hk * A_GROUP + g
            slope = 2.0 ** (-8.0 * (head + 1) / A_HEADS)
            q = qa_ref[:, head * HEAD_DIM:(head + 1) * HEAD_DIM]
            qn = _head_norm(q, qg_ref[...], HEAD_DIM ** -0.5).astype(BF16)
            s = _dot_nt(qn, kn)
            s = jnp.where(valid, s - slope * dist_f, NEG_INF)
            sink = sink_ref[head]
            m = jnp.maximum(jnp.max(s, axis=-1, keepdims=True), sink)
            e = jnp.exp(s - m)
            denom = jnp.sum(e, axis=-1, keepdims=True) + jnp.exp(sink - m)
            o = _dot(e.astype(BF16), vb) / denom
            o_ref[:, head * HEAD_DIM:(head + 1) * HEAD_DIM] = o


def _win_gqa(qa, kva, q_gain, k_gain, sink):
    b, seq_len, _ = qa.shape
    return pl.pallas_call(
        functools.partial(_win_gqa_kernel, seq_len=seq_len),
        out_shape=jax.ShapeDtypeStruct((b, seq_len, A_Q), F32),
        grid=(b, seq_len // WIN),
        in_specs=[
            pl.BlockSpec(memory_space=pltpu.SMEM),
            pl.BlockSpec((None, WIN, A_Q), lambda i, n: (i, n, 0)),
            pl.BlockSpec((None, seq_len, 2 * A_KV), lambda i, n: (i, 0, 0)),
            _const_spec((1, HEAD_DIM)),
            _const_spec((1, HEAD_DIM)),
        ],
        out_specs=pl.BlockSpec((None, WIN, A_Q), lambda i, n: (i, n, 0)),
        compiler_params=_params("parallel", "arbitrary"),
        name="win_gqa",
    )(sink, qa, kva, q_gain, k_gain)


def _na_first_kblk(m):
    return jnp.clip(2 * m - 1, 0, (GRID_W // NA_KBLK_ROWS) - NA_KBLKS)


def _nbr_attn_kernel(*refs, n_rows):
    qb_ref = refs[0]
    k_refs = refs[1:1 + NA_KBLKS]
    v_refs = refs[1 + NA_KBLKS:1 + 2 * NA_KBLKS]
    qg_ref, kg_ref, bias_ref, o_ref, kn_s, v_s = refs[1 + 2 * NA_KBLKS:]
    m = pl.program_id(1)
    blk = NA_KBLK_ROWS * GRID_W
    first_row = NA_KBLK_ROWS * jnp.clip(2 * m - 1, 0, n_rows // NA_KBLK_ROWS - NA_KBLKS)
    for i in range(NA_KBLKS):
        kblk = k_refs[i][...]
        for h in range(B_HEADS):
            seg = kblk[:, h * HEAD_DIM:(h + 1) * HEAD_DIM]
            kn_s[i * blk:(i + 1) * blk, h * HEAD_DIM:(h + 1) * HEAD_DIM] = (
                _head_norm(seg, kg_ref[...]).astype(BF16))
        v_s[i * blk:(i + 1) * blk, :] = v_refs[i][...].astype(BF16)

    def row_body(i, carry):
        r = NA_QROWS * m + i
        rs = jnp.clip(r - NA_ROWS // 2, 0, n_rows - NA_ROWS)
        off = pl.multiple_of((rs - first_row) * GRID_W, GRID_W)
        var = r - rs
        q_row = qb_ref[pl.ds(pl.multiple_of(i * GRID_W, GRID_W), GRID_W), :]
        for h in range(B_HEADS):
            lanes = slice(h * HEAD_DIM, (h + 1) * HEAD_DIM)
            qn = _head_norm(q_row[:, lanes], qg_ref[...], HEAD_DIM ** -0.5).astype(BF16)
            k = kn_s[pl.ds(off, NA_ROWS * GRID_W), lanes]
            v = v_s[pl.ds(off, NA_ROWS * GRID_W), lanes]
            s = _dot_nt(qn, k) + bias_ref[h, var]
            e = jnp.exp(s - jnp.max(s, axis=-1, keepdims=True))
            o = _dot(e.astype(BF16), v) / jnp.sum(e, axis=-1, keepdims=True)
            o_ref[pl.ds(pl.multiple_of(i * GRID_W, GRID_W), GRID_W), lanes] = o
        return carry

    lax.fori_loop(0, NA_QROWS, row_body, 0)


def _nbr_bias_table(rpb):
    qc = jnp.arange(GRID_W)[:, None]
    kc = jnp.arange(GRID_W)[None, :]
    cs = jnp.clip(qc - NA_COLS // 2, 0, GRID_W - NA_COLS)
    ok = (kc >= cs) & (kc < cs + NA_COLS)
    dc = jnp.clip(kc - qc + NA_COLS - 1, 0, 2 * NA_COLS - 2)
    var = jnp.arange(NA_ROWS)[:, None]
    ki = jnp.arange(NA_ROWS)[None, :]
    dr = ki - var + NA_ROWS - 1
    tab = rpb.astype(F32)[:, dr[:, :, None, None], dc[None, None, :, :]]
    tab = jnp.where(ok[None, None, None], tab, NEG_INF)
    return jnp.transpose(tab, (0, 1, 3, 2, 4)).reshape(B_HEADS, NA_ROWS, GRID_W, NA_ROWS * GRID_W)


def _nbr_attn(qb, kb, vb, q_gain, k_gain, bias_tab):
    b, seq_len, _ = qb.shape
    n_rows = seq_len // GRID_W
    assert n_rows >= NA_KBLKS * NA_KBLK_ROWS and n_rows % NA_QROWS == 0
    qtile = NA_QROWS * GRID_W
    blk = NA_KBLK_ROWS * GRID_W
    last = n_rows // NA_KBLK_ROWS - NA_KBLKS

    def kv_spec(i):
        return pl.BlockSpec((None, blk, B_QKV), lambda s, m: (s, jnp.clip(2 * m - 1, 0, last) + i, 0))

    return pl.pallas_call(
        functools.partial(_nbr_attn_kernel, n_rows=n_rows),
        out_shape=jax.ShapeDtypeStruct((b, seq_len, B_QKV), F32),
        grid=(b, n_rows // NA_QROWS),
        in_specs=(
            [pl.BlockSpec((None, qtile, B_QKV), lambda s, m: (s, m, 0))]
            + [kv_spec(i) for i in range(NA_KBLKS)]
            + [kv_spec(i) for i in range(NA_KBLKS)]
            + [_const_spec((1, HEAD_DIM)), _const_spec((1, HEAD_DIM)),
               _const_spec((B_HEADS, NA_ROWS, GRID_W, NA_ROWS * GRID_W))]
        ),
        out_specs=pl.BlockSpec((None, qtile, B_QKV), lambda s, m: (s, m, 0)),
        scratch_shapes=[pltpu.VMEM((NA_KBLKS * blk, B_QKV), BF16),
                        pltpu.VMEM((NA_KBLKS * blk, B_QKV), BF16)],
        compiler_params=_params("parallel", "arbitrary"),
        name="nbr_attn",
    )(qb, *([kb] * NA_KBLKS), *([vb] * NA_KBLKS), q_gain, k_gain, bias_tab)


def _mix_out_kernel(x_ref, ya_ref, yb_ref, w_ref, o_ref):
    half = ya_ref.shape[-1]
    o_ref[...] = (x_ref[...]
                  + _dot(ya_ref[...].astype(BF16), w_ref[:half, :])
                  + _dot(yb_ref[...].astype(BF16), w_ref[half:, :]))


def _attn_out(x2d, ya2d, yb2d, w):
    t = x2d.shape[0]
    return pl.pallas_call(
        _mix_out_kernel,
        out_shape=jax.ShapeDtypeStruct(x2d.shape, F32),
        grid=(t // TOKEN_TILE,),
        in_specs=[
            pl.BlockSpec((TOKEN_TILE, D_MODEL), lambda i: (i, 0)),
            pl.BlockSpec((TOKEN_TILE, A_Q), lambda i: (i, 0)),
            pl.BlockSpec((TOKEN_TILE, B_QKV), lambda i: (i, 0)),
            _const_spec((A_Q + B_QKV, D_MODEL)),
        ],
        out_specs=pl.BlockSpec((TOKEN_TILE, D_MODEL), lambda i: (i, 0)),
        compiler_params=_params("parallel"),
        name="attn_out",
    )(x2d, ya2d, yb2d, w)


def _ssm_in_kernel(x_ref, g_ref, w_ref, uc_ref, adgd_ref):
    h = _dot(_rms(x_ref[...], g_ref[...]).astype(BF16), w_ref[...])
    uc_ref[...] = h[:, :C_WIDTH]
    adgd_ref[...] = h[:, C_WIDTH:]


def _ssm_in(x2d, g, w):
    t = x2d.shape[0]
    widths = (C_WIDTH, 2 * D_WIDTH)
    return pl.pallas_call(
        _ssm_in_kernel,
        out_shape=[jax.ShapeDtypeStruct((t, n), F32) for n in widths],
        grid=(t // TOKEN_TILE,),
        in_specs=[
            pl.BlockSpec((TOKEN_TILE, D_MODEL), lambda i: (i, 0)),
            _const_spec((1, D_MODEL)),
            _const_spec((D_MODEL, C_WIDTH + 2 * D_WIDTH)),
        ],
        out_specs=[pl.BlockSpec((TOKEN_TILE, n), lambda i: (i, 0)) for n in widths],
        compiler_params=_params("parallel"),
        name="ssm_in",
    )(x2d, g, w)


SLAB_GROUPS = LANES // SSM_GROUP_CH
N_SLABS = SSM_GROUPS // SLAB_GROUPS
SLAB_HALF = SLAB_GROUPS * SSM_STATE
SLAB = 2 * SLAB_HALF
STATE_W = N_SLABS * SLAB


def _ssm_scan_kernel(uf_ref, ub_ref, b_ref, a_ref, c_ref, yf_ref, yb_ref, s_ref, h_ref):
    c = pl.program_id(1)
    rows = SCAN_CHUNK * SCAN_SEQS

    @pl.when(c == 0)
    def _():
        h_ref[...] = jnp.zeros_like(h_ref)

    for d, u_ref in enumerate((uf_ref, ub_ref)):
        u2d = u_ref[...].reshape(rows, C_WIDTH).astype(BF16)
        for j in range(N_SLABS):
            bu = _dot(u2d[:, j * LANES:(j + 1) * LANES], b_ref[d, j])
            s_ref[d, :, :, j * SLAB:(j + 1) * SLAB] = bu.reshape(SCAN_CHUNK, SCAN_SEQS, SLAB)

    for pair in range(N_SLABS // 2):
        slabs = (2 * pair, 2 * pair + 1)

        def step(i, carry):
            new = []
            for d in range(2):
                t = i if d == 0 else SCAN_CHUNK - 1 - i
                for idx, j in enumerate(slabs):
                    hr, hi = carry[2 * (2 * d + idx)], carry[2 * (2 * d + idx) + 1]
                    re = slice(j * SLAB, j * SLAB + SLAB_HALF)
                    im = slice(j * SLAB + SLAB_HALF, (j + 1) * SLAB)
                    ar = a_ref[d, :, re]
                    ai = a_ref[d, :, im]
                    nr = ar * hr - ai * hi + s_ref[d, t, :, re]
                    ni = ar * hi + ai * hr + s_ref[d, t, :, im]
                    s_ref[d, t, :, re] = nr
                    s_ref[d, t, :, im] = ni
                    new += [nr, ni]
            return tuple(new)

        init = []
        for d in range(2):
            for j in slabs:
                init += [h_ref[d, :, j * SLAB:j * SLAB + SLAB_HALF],
                         h_ref[d, :, j * SLAB + SLAB_HALF:(j + 1) * SLAB]]
        final = lax.fori_loop(0, SCAN_CHUNK, step, tuple(init))
        k = 0
        for d in range(2):
            for j in slabs:
                h_ref[d, :, j * SLAB:j * SLAB + SLAB_HALF] = final[k]
                h_ref[d, :, j * SLAB + SLAB_HALF:(j + 1) * SLAB] = final[k + 1]
                k += 2

    for d, y_ref in enumerate((yf_ref, yb_ref)):
        for j in range(N_SLABS):
            hs = s_ref[d, :, :, j * SLAB:(j + 1) * SLAB].reshape(rows, SLAB).astype(BF16)
            y = _dot(hs, c_ref[d, j])
            y_ref[:, :, j * LANES:(j + 1) * LANES] = y.reshape(SCAN_CHUNK, SCAN_SEQS, LANES)


def _ssm_scan_weights(lam_re, lam_im, log_dt, b_re, b_im, c_re, c_im):
    lam = lax.complex(lam_re.astype(F32), lam_im.astype(F32))
    dt = jnp.exp(log_dt.astype(F32))[:, :, None]
    lam_bar = jnp.exp(lam * dt)
    b_bar = ((lam_bar - 1.0) / lam)[..., None] * lax.complex(b_re.astype(F32), b_im.astype(F32))
    eye = jnp.eye(SLAB_GROUPS, dtype=F32)

    def pack_b(part):
        x = part.reshape(2, N_SLABS, SLAB_GROUPS, SSM_STATE, SSM_GROUP_CH)
        x = jnp.einsum('dsgnp,gh->dsgphn', x, eye)
        return x.reshape(2, N_SLABS, LANES, SLAB_HALF)

    b_pack = jnp.concatenate([pack_b(jnp.real(b_bar)), pack_b(jnp.imag(b_bar))], axis=-1).astype(BF16)

    def pack_a(part):
        return part.reshape(2, N_SLABS, SLAB_HALF)

    a_pack = jnp.concatenate([pack_a(jnp.real(lam_bar)), pack_a(jnp.imag(lam_bar))], axis=-1)
    a_pack = jnp.broadcast_to(a_pack.reshape(2, 1, STATE_W), (2, SCAN_SEQS, STATE_W))

    def pack_c(part):
        x = part.astype(F32).reshape(2, N_SLABS, SLAB_GROUPS, SSM_GROUP_CH, SSM_STATE)
        x = jnp.einsum('dsgpn,gh->dsgnhp', x, eye)
        return x.reshape(2, N_SLABS, SLAB_HALF, LANES)

    c_pack = jnp.concatenate([pack_c(c_re), -pack_c(c_im)], axis=2).astype(BF16)
    return b_pack, a_pack, c_pack


def _ssm_scan(u_tm, b_pack, a_pack, c_pack):
    seq_len, b, _ = u_tm.shape
    nc = seq_len // SCAN_CHUNK
    blk = (SCAN_CHUNK, SCAN_SEQS, C_WIDTH)
    return pl.pallas_call(
        _ssm_scan_kernel,
        out_shape=[jax.ShapeDtypeStruct(u_tm.shape, F32)] * 2,
        grid=(b // SCAN_SEQS, nc),
        in_specs=[
            pl.BlockSpec(blk, lambda s, c: (c, s, 0)),
            pl.BlockSpec(blk, lambda s, c: (nc - 1 - c, s, 0)),
            _const_spec((2, N_SLABS, LANES, SLAB)),
            _const_spec((2, SCAN_SEQS, STATE_W)),
            _const_spec((2, N_SLABS, SLAB, LANES)),
        ],
        out_specs=[
            pl.BlockSpec(blk, lambda s, c: (c, s, 0)),
            pl.BlockSpec(blk, lambda s, c: (nc - 1 - c, s, 0)),
        ],
        scratch_shapes=[pltpu.VMEM((2, SCAN_CHUNK, SCAN_SEQS, STATE_W), F32),
                        pltpu.VMEM((2, SCAN_SEQS, STATE_W), F32)],
        compiler_params=_params("parallel", "arbitrary"),
        name="ssm_scan",
    )(u_tm, u_tm, b_pack, a_pack, c_pack)


def _gelu_tanh(x):
    return 0.5 * x * (1.0 + jnp.tanh(math.sqrt(2.0 / math.pi) * (x + 0.044715 * (x * x * x))))


def _ssm_out_kernel(x_ref, yf_ref, yb_ref, uc_ref, cur_ref, prev_ref, next_ref,
                    dskip_ref, wglu_ref, bglu_ref, cw_ref, cb_ref, lng_ref, lnb_ref, w_ref,
                    o_ref, pad_ref, yd_ref):
    i = pl.program_id(1)
    last = pl.num_programs(1) - 1
    tile = cur_ref.shape[0]

    z = _gelu_tanh(yf_ref[...] + yb_ref[...] + dskip_ref[...] * uc_ref[...])
    yc = z * jax.nn.sigmoid(_dot(z.astype(BF16), wglu_ref[...]) + bglu_ref[...])

    def glu(ref):
        v = ref[...]
        return v[:, :D_WIDTH] * jax.nn.sigmoid(v[:, D_WIDTH:])

    pad_ref[0:CONV_HALO, :] = jnp.where(i > 0, glu(prev_ref), 0.0)
    pad_ref[CONV_HALO:CONV_HALO + tile, :] = glu(cur_ref)
    pad_ref[CONV_HALO + tile:, :] = jnp.where(i < last, glu(next_ref), 0.0)
    shift = CONV_HALO - CONV_WIDTH // 2
    for r0 in range(0, tile, CONV_ROWS):
        acc = jnp.zeros((CONV_ROWS, D_WIDTH), F32) + cb_ref[...]
        for k in range(CONV_WIDTH):
            acc = acc + cw_ref[k:k + 1, :] * pad_ref[r0 + k + shift:r0 + k + shift + CONV_ROWS, :]
        mu = jnp.mean(acc, axis=-1, keepdims=True)
        xc = acc - mu
        y = xc * lax.rsqrt(jnp.mean(xc * xc, axis=-1, keepdims=True) + EPS) * lng_ref[...] + lnb_ref[...]
        yd_ref[r0:r0 + CONV_ROWS, :] = y * jax.nn.sigmoid(y)

    o_ref[...] = (x_ref[...]
                  + _dot(yc.astype(BF16), w_ref[:C_WIDTH, :])
                  + _dot(yd_ref[...].astype(BF16), w_ref[C_WIDTH:, :]))


def _ssm_out(x, yf, yb, uc, adgd, d_skip, w_glu, b_glu, conv_w, conv_b, ln_g, ln_b, w_out):
    b, seq_len, _ = x.shape
    tile = TOKEN_TILE
    per = tile // CONV_HALO
    n_halo = seq_len // CONV_HALO

    def tok(width):
        return pl.BlockSpec((None, tile, width), lambda s, i: (s, i, 0))

    return pl.pallas_call(
        _ssm_out_kernel,
        out_shape=jax.ShapeDtypeStruct(x.shape, F32),
        grid=(b, seq_len // tile),
        in_specs=[
            tok(D_MODEL), tok(C_WIDTH), tok(C_WIDTH), tok(C_WIDTH), tok(2 * D_WIDTH),
            pl.BlockSpec((None, CONV_HALO, 2 * D_WIDTH), lambda s, i: (s, jnp.maximum(i * per - 1, 0), 0)),
            pl.BlockSpec((None, CONV_HALO, 2 * D_WIDTH),
                         lambda s, i: (s, jnp.minimum((i + 1) * per, n_halo - 1), 0)),
            _const_spec((1, C_WIDTH)),
            _const_spec((C_WIDTH, C_WIDTH)),
            _const_spec((1, C_WIDTH)),
            _const_spec((CONV_WIDTH, D_WIDTH)),
            _const_spec((1, D_WIDTH)),
            _const_spec((1, D_WIDTH)),
            _const_spec((1, D_WIDTH)),
            _const_spec((C_WIDTH + D_WIDTH, D_MODEL)),
        ],
        out_specs=tok(D_MODEL),
        scratch_shapes=[pltpu.VMEM((tile + 2 * CONV_HALO, D_WIDTH), F32),
                        pltpu.VMEM((tile, D_WIDTH), F32)],
        compiler_params=_params("parallel", "arbitrary"),
        name="ssm_out",
    )(x, yf, yb, uc, adgd, adgd, adgd, d_skip, w_glu, b_glu, conv_w, conv_b, ln_g, ln_b, w_out)


def _row(v):
    return v.astype(F32).reshape(1, -1)


def _attn_layer(x, w, j):
    b, seq_len, _ = x.shape
    x2d = x.reshape(b * seq_len, D_MODEL)
    qa, kva, qb, kb, vb = _attn_in(x2d, w['norm_mix_row'], w['w_attn_in'][j])
    as3d = lambda a: a.reshape(b, seq_len, a.shape[-1])
    ya = _win_gqa(as3d(qa), as3d(kva), _row(w['q_gain_a'][j]), _row(w['k_gain_a'][j]),
                  w['sink_a'][j].astype(F32))
    yb = _nbr_attn(as3d(qb), as3d(kb), as3d(vb), _row(w['q_gain_b'][j]), _row(w['k_gain_b'][j]),
                   w['nbr_bias'][j])
    out = _attn_out(x2d, ya.reshape(b * seq_len, A_Q), yb.reshape(b * seq_len, B_QKV), w['w_attn_out'][j])
    return out.reshape(x.shape)


def _ssm_layer(x, w, j):
    b, seq_len, _ = x.shape
    x2d = x.reshape(b * seq_len, D_MODEL)
    uc, adgd = _ssm_in(x2d, w['norm_mix_row'], w['w_ssm_in'][j])
    uc = uc.reshape(b, seq_len, C_WIDTH)
    b_pack, a_pack, c_pack = w['scan'][j]
    yf, yb = _ssm_scan(jnp.transpose(uc, (1, 0, 2)), b_pack, a_pack, c_pack)
    yf = jnp.transpose(yf, (1, 0, 2))
    yb = jnp.transpose(yb, (1, 0, 2))
    return _ssm_out(x, yf, yb, uc, adgd.reshape(b, seq_len, 2 * D_WIDTH),
                    _row(w['d_skip'][j]), w['w_glu_c'][j], _row(w['b_glu_c'][j]),
                    w['conv_w'][j].astype(F32), _row(w['conv_b'][j]), _row(w['ln_g_d'][j]),
                    _row(w['ln_b_d'][j]), w['w_ssm_out'][j])


def _trunk(x, p, w):
    b, seq_len, _ = x.shape
    assert seq_len % TOKEN_TILE == 0 and seq_len % SCAN_CHUNK == 0 and b % SCAN_SEQS == 0
    assert seq_len >= 3 * WIN
    p3d = p.reshape(DEPTH, b * seq_len, PLE_DIM)
    for i in range(DEPTH):
        x2d = x.reshape(b * seq_len, D_MODEL)
        x2d = _ffn(x2d, _row(w['norm_ffn1'][i]), w['w_ffn1_in'][i], w['w_ffn1_out'][i])
        x = x2d.reshape(b, seq_len, D_MODEL)
        lw = dict(w, norm_mix_row=_row(w['norm_mix'][i]))
        x = _attn_layer(x, lw, i // 2) if i % 2 == 0 else _ssm_layer(x, lw, i // 2)
        x2d = x.reshape(b * seq_len, D_MODEL)
        x2d = _ffn(x2d, _row(w['norm_ffn2'][i]), w['w_ffn2_in'][i], w['w_ffn2_out'][i])
        x2d = _ple(x2d, p3d, i, _row(w['norm_ple'][i]), w['w_ple_gate'][i], w['w_ple_proj'][i],
                   _row(w['norm_ple_post'][i]))
        x = x2d.reshape(b, seq_len, D_MODEL)
    return x


def kernel(x_prompt, x_sample, p_prompt, p_sample, norm_ffn1, w_ffn1_in, w_ffn1_out, norm_mix, norm_ffn2, w_ffn2_in, w_ffn2_out, norm_ple, w_ple_gate, w_ple_proj, norm_ple_post, w_attn_in, q_gain_a, k_gain_a, sink_a, q_gain_b, k_gain_b, rpb_b, w_attn_out, w_ssm_in, lam_re, lam_im, log_dt, b_re, b_im, c_re, c_im, d_skip, w_glu_c, b_glu_c, conv_w, conv_b, ln_g_d, ln_b_d, w_ssm_out):
    w = dict(norm_ffn1=norm_ffn1, norm_mix=norm_mix, norm_ffn2=norm_ffn2, norm_ple=norm_ple,
             norm_ple_post=norm_ple_post, q_gain_a=q_gain_a, k_gain_a=k_gain_a, sink_a=sink_a,
             q_gain_b=q_gain_b, k_gain_b=k_gain_b, d_skip=d_skip, b_glu_c=b_glu_c, conv_w=conv_w,
             conv_b=conv_b, ln_g_d=ln_g_d, ln_b_d=ln_b_d)
    for name, val in dict(w_ffn1_in=w_ffn1_in, w_ffn1_out=w_ffn1_out, w_ffn2_in=w_ffn2_in,
                          w_ffn2_out=w_ffn2_out, w_ple_gate=w_ple_gate, w_ple_proj=w_ple_proj,
                          w_attn_in=w_attn_in, w_attn_out=w_attn_out, w_ssm_in=w_ssm_in,
                          w_glu_c=w_glu_c, w_ssm_out=w_ssm_out).items():
        w[name] = val.astype(BF16)
    w['nbr_bias'] = [_nbr_bias_table(rpb_b[j]) for j in range(rpb_b.shape[0])]
    w['scan'] = [_ssm_scan_weights(lam_re[j], lam_im[j], log_dt[j], b_re[j], b_im[j], c_re[j], c_im[j])
                 for j in range(lam_re.shape[0])]
    return (_trunk(x_prompt, p_prompt, w), _trunk(x_sample, p_sample, w))
```

```python
import functools
import math

import jax
import jax.numpy as jnp
from jax import lax
from jax.experimental import pallas as pl
from jax.experimental.pallas import tpu as pltpu

D_MODEL = 1024
DEPTH = 4
HEAD_DIM = 64
A_HEADS = 8
A_KV_HEADS = 2
A_GROUP = A_HEADS // A_KV_HEADS
WIN = 128
B_HEADS = 8
GRID_W = 64
NA_ROWS = 8
NA_COLS = 16
C_WIDTH = 512
SSM_GROUP_CH = 16
SSM_GROUPS = C_WIDTH // SSM_GROUP_CH
SSM_STATE = 64
D_WIDTH = 512
CONV_WIDTH = 31
D_FF = 2816
PLE_DIM = 256
A_Q = A_HEADS * HEAD_DIM
A_KV = A_KV_HEADS * HEAD_DIM
B_QKV = B_HEADS * HEAD_DIM
ATTN_IN = A_Q + 2 * A_KV + 3 * B_QKV
NEG_INF = -1e30
EPS = 1e-6

BF16 = jnp.bfloat16
F32 = jnp.float32

VMEM_LIMIT_BYTES = 52 * 1024 * 1024
LANES = 128
SUBLANES = 8
TOKEN_TILE = 512
FF_CHUNK = 256
SCAN_CHUNK = 64
SCAN_SEQS = SUBLANES
NA_QROWS = 8
CONV_HALO = 16
CONV_ROWS = 64


def _params(*sem):
    return pltpu.CompilerParams(dimension_semantics=sem, vmem_limit_bytes=VMEM_LIMIT_BYTES)


def _rms(x, g):
    return x * lax.rsqrt(jnp.mean(x * x, axis=-1, keepdims=True) + EPS) * g


def _dot(a, b):
    return jnp.dot(a, b, preferred_element_type=F32)


def _dot_nt(a, b):
    return lax.dot_general(a, b, (((1,), (1,)), ((), ())), preferred_element_type=F32)


def _const_spec(shape):
    nd = len(shape)
    return pl.BlockSpec(shape, lambda *_: (0,) * nd)


def _ffn_kernel(x_ref, g_ref, win_ref, wout_ref, o_ref, acc_ref):
    x = x_ref[...]
    xn = _rms(x, g_ref[...]).astype(BF16)
    for c in range(D_FF // FF_CHUNK):
        lo = c * FF_CHUNK
        gate = _dot(xn, win_ref[:, lo:lo + FF_CHUNK])
        up = _dot(xn, win_ref[:, D_FF + lo:D_FF + lo + FF_CHUNK])
        act = (gate * jax.nn.sigmoid(gate) * up).astype(BF16)
        part = _dot(act, wout_ref[lo:lo + FF_CHUNK, :])
        if c == 0:
            acc_ref[...] = part
        else:
            acc_ref[...] += part
    o_ref[...] = x + 0.5 * acc_ref[...]


def _ffn(x2d, g, w_in, w_out):
    t = x2d.shape[0]
    return pl.pallas_call(
        _ffn_kernel,
        out_shape=jax.ShapeDtypeStruct(x2d.shape, F32),
        grid=(t // TOKEN_TILE,),
        in_specs=[
            pl.BlockSpec((TOKEN_TILE, D_MODEL), lambda i: (i, 0)),
            _const_spec((1, D_MODEL)),
            _const_spec((D_MODEL, 2 * D_FF)),
            _const_spec((D_FF, D_MODEL)),
        ],
        out_specs=pl.BlockSpec((TOKEN_TILE, D_MODEL), lambda i: (i, 0)),
        scratch_shapes=[pltpu.VMEM((TOKEN_TILE, D_MODEL), F32)],
        compiler_params=_params("parallel"),
        name="ffn",
    )(x2d, g, w_in, w_out)


def _ple_kernel(x_ref, p_ref, g1_ref, wg_ref, wp_ref, g2_ref, o_ref):
    x = x_ref[...]
    gate = jax.nn.sigmoid(_dot(_rms(x, g1_ref[...]).astype(BF16), wg_ref[...]))
    proj = _dot(p_ref[...].astype(BF16), wp_ref[...])
    o_ref[...] = x + gate * _rms(proj, g2_ref[...])


def _ple(x2d, p3d, layer, g1, wg, wp, g2):
    t = x2d.shape[0]
    return pl.pallas_call(
        _ple_kernel,
        out_shape=jax.ShapeDtypeStruct(x2d.shape, F32),
        grid=(t // TOKEN_TILE,),
        in_specs=[
            pl.BlockSpec((TOKEN_TILE, D_MODEL), lambda i: (i, 0)),
            pl.BlockSpec((None, TOKEN_TILE, PLE_DIM), lambda i: (layer, i, 0)),
            _const_spec((1, D_MODEL)),
            _const_spec((D_MODEL, D_MODEL)),
            _const_spec((PLE_DIM, D_MODEL)),
            _const_spec((1, D_MODEL)),
        ],
        out_specs=pl.BlockSpec((TOKEN_TILE, D_MODEL), lambda i: (i, 0)),
        compiler_params=_params("parallel"),
        name="ple",
    )(x2d, p3d, g1, wg, wp, g2)


ATTN_CHUNK = 512
ATTN_COLS = 5 * ATTN_CHUNK
ATTN_NORM_PIECES = (0, 1, 2, 4, 5, 6, 7)
NORM_PIECE = 256


def _attn_in_kernel(x_ref, g_ref, w_ref, gain_ref, ones_ref, qa_ref, kva_ref, qb_ref, kb_ref, vb_ref):
    xn = _rms(x_ref[...], g_ref[...]).astype(BF16)
    for c, ref in enumerate((qa_ref, kva_ref, qb_ref, kb_ref, vb_ref)):
        h = _dot(xn, w_ref[:, c * ATTN_CHUNK:(c + 1) * ATTN_CHUNK])
        for half in range(ATTN_CHUNK // NORM_PIECE):
            piece = c * (ATTN_CHUNK // NORM_PIECE) + half
            lanes = slice(half * NORM_PIECE, (half + 1) * NORM_PIECE)
            hp = h[:, lanes]
            if piece in ATTN_NORM_PIECES:
                ss = _dot((hp * hp).astype(BF16), ones_ref[...])
                hp = hp * lax.rsqrt(ss * (1.0 / HEAD_DIM) + EPS) * gain_ref[:, piece * NORM_PIECE:(piece + 1) * NORM_PIECE]
            ref[:, lanes] = hp.astype(BF16)


def _attn_in_weights(w, q_gain_a, k_gain_a, q_gain_b, k_gain_b):
    ka = [w[:, A_Q + h * HEAD_DIM:A_Q + (h + 1) * HEAD_DIM] for h in range(A_KV_HEADS)]
    va = [w[:, A_Q + A_KV + h * HEAD_DIM:A_Q + A_KV + (h + 1) * HEAD_DIM] for h in range(A_KV_HEADS)]
    w2 = jnp.concatenate([w[:, :A_Q]] + [c for h in range(A_KV_HEADS) for c in (ka[h], ka[h])]
                         + [c for h in range(A_KV_HEADS) for c in (va[h], va[h])]
                         + [w[:, A_Q + 2 * A_KV:]], axis=1).astype(BF16)
    scale = HEAD_DIM ** -0.5
    f = lambda g, reps, s=1.0: jnp.tile(g.astype(F32) * s, reps)
    gain = jnp.concatenate([f(q_gain_a, A_HEADS, scale), f(k_gain_a, 2 * A_KV_HEADS), jnp.ones((2 * A_KV,), F32),
                            f(q_gain_b, B_HEADS, scale), f(k_gain_b, B_HEADS), jnp.ones((B_QKV,), F32)])
    return w2, gain.reshape(1, ATTN_COLS)


def _head_ones():
    idx = jnp.arange(NORM_PIECE) // HEAD_DIM
    return (idx[:, None] == idx[None, :]).astype(BF16)


def _attn_in(x2d, g, w2, gain):
    t = x2d.shape[0]
    return pl.pallas_call(
        _attn_in_kernel,
        out_shape=[jax.ShapeDtypeStruct((t, ATTN_CHUNK), BF16)] * 5,
        grid=(t // TOKEN_TILE,),
        in_specs=[
            pl.BlockSpec((TOKEN_TILE, D_MODEL), lambda i: (i, 0)),
            _const_spec((1, D_MODEL)),
            _const_spec((D_MODEL, ATTN_COLS)),
            _const_spec((1, ATTN_COLS)),
            _const_spec((NORM_PIECE, NORM_PIECE)),
        ],
        out_specs=[pl.BlockSpec((TOKEN_TILE, ATTN_CHUNK), lambda i: (i, 0))] * 5,
        compiler_params=_params("parallel"),
        name="attn_in",
    )(x2d, g, w2, gain, _head_ones())


def _pair_rows(q_tile):
    upper = lax.broadcasted_iota(jnp.int32, q_tile.shape, 1) >= HEAD_DIM
    zero = jnp.zeros_like(q_tile)
    return jnp.concatenate([jnp.where(upper, zero, q_tile), jnp.where(upper, q_tile, zero)], axis=0)


def _unpair_rows(o2):
    r = o2.shape[0] // 2
    upper = lax.broadcasted_iota(jnp.int32, (r, LANES), 1) >= HEAD_DIM
    return jnp.where(upper, o2[r:], o2[:r])


def _win_gqa_kernel(sink_ref, qa_ref, kva_ref, o_ref, *, seq_len):
    n = pl.program_id(1)
    span = 3 * WIN
    start = pl.multiple_of(jnp.clip((n - 1) * WIN, 0, seq_len - span), WIN)
    kv = kva_ref[pl.ds(start, span), :]
    rows = lax.broadcasted_iota(jnp.int32, (2 * WIN, span), 0)
    second = rows >= WIN
    t_pos = n * WIN + jnp.where(second, rows - WIN, rows)
    s_pos = start + lax.broadcasted_iota(jnp.int32, (2 * WIN, span), 1)
    dist = jnp.abs(s_pos - t_pos)
    valid = dist <= WIN
    dist_f = dist.astype(F32)
    second_col = second[:, :1]
    for hk in range(A_KV_HEADS):
        k2 = kv[:, hk * LANES:(hk + 1) * LANES]
        v2 = kv[:, 2 * A_KV + hk * LANES:2 * A_KV + (hk + 1) * LANES]
        for jj in range(A_GROUP // 2):
            j = hk * (A_GROUP // 2) + jj
            h0, h1 = 2 * j, 2 * j + 1
            slope = jnp.where(second_col, 2.0 ** (-8.0 * (h1 + 1) / A_HEADS), 2.0 ** (-8.0 * (h0 + 1) / A_HEADS))
            sink = jnp.where(second_col, sink_ref[h1], sink_ref[h0])
            s = _dot_nt(_pair_rows(qa_ref[:, j * LANES:(j + 1) * LANES]), k2)
            s = jnp.where(valid, s - slope * dist_f, NEG_INF)
            m = jnp.maximum(jnp.max(s, axis=-1, keepdims=True), sink)
            e = jnp.exp(s - m)
            denom = jnp.sum(e, axis=-1, keepdims=True) + jnp.exp(sink - m)
            o2 = _dot(e.astype(BF16), v2) / denom
            o_ref[:, j * LANES:(j + 1) * LANES] = _unpair_rows(o2).astype(BF16)


def _win_gqa(qa, kva, sink):
    b, seq_len, _ = qa.shape
    return pl.pallas_call(
        functools.partial(_win_gqa_kernel, seq_len=seq_len),
        out_shape=jax.ShapeDtypeStruct((b, seq_len, A_Q), BF16),
        grid=(b, seq_len // WIN),
        in_specs=[
            pl.BlockSpec(memory_space=pltpu.SMEM),
            pl.BlockSpec((None, WIN, A_Q), lambda i, n: (i, n, 0)),
            pl.BlockSpec((None, seq_len, ATTN_CHUNK), lambda i, n: (i, 0, 0)),
        ],
        out_specs=pl.BlockSpec((None, WIN, A_Q), lambda i, n: (i, n, 0)),
        compiler_params=_params("parallel", "arbitrary"),
        name="win_gqa",
    )(sink, qa, kva)


def _nbr_attn_kernel(qb_ref, kb_ref, vb_ref, bias_ref, o_ref, *, n_rows):
    m = pl.program_id(1)
    keys = NA_ROWS * GRID_W

    def row_body(i, carry):
        r = NA_QROWS * m + i
        rs = jnp.clip(r - NA_ROWS // 2, 0, n_rows - NA_ROWS)
        off = pl.multiple_of(rs * GRID_W, GRID_W)
        var = r - rs
        q_off = pl.multiple_of(i * GRID_W, GRID_W)
        for pr in range(B_HEADS // 2):
            lanes = slice(pr * LANES, (pr + 1) * LANES)
            q2 = _pair_rows(qb_ref[pl.ds(q_off, GRID_W), lanes])
            s = _dot_nt(q2, kb_ref[pl.ds(off, keys), lanes]) + bias_ref[pr, var]
            e = jnp.exp(s - jnp.max(s, axis=-1, keepdims=True))
            o2 = _dot(e.astype(BF16), vb_ref[pl.ds(off, keys), lanes]) / jnp.sum(e, axis=-1, keepdims=True)
            o_ref[pl.ds(q_off, GRID_W), lanes] = _unpair_rows(o2).astype(BF16)
        return carry

    lax.fori_loop(0, NA_QROWS, row_body, 0)


def _nbr_bias_table(rpb):
    keys = NA_ROWS * GRID_W
    z = jnp.pad(rpb.astype(F32), ((0, 0), (0, 0), (0, GRID_W - (2 * NA_COLS - 1)))).reshape(B_HEADS, -1)
    z = jnp.stack([z[:, (NA_ROWS - 1 - v) * GRID_W:(NA_ROWS - 1 - v) * GRID_W + keys] for v in range(NA_ROWS)], axis=1)
    period = keys + GRID_W
    zz = jnp.concatenate([z[..., NA_COLS - 1:], jnp.zeros((B_HEADS, NA_ROWS, period - keys), F32), z[..., :NA_COLS - 1]], axis=-1)
    skew = jnp.tile(zz, (1, 1, GRID_W))[..., :GRID_W * (period - 1)].reshape(B_HEADS, NA_ROWS, GRID_W, period - 1)
    tab = skew[..., :keys]
    qc = jnp.arange(GRID_W)[:, None]
    kc = jnp.arange(keys)[None, :] % GRID_W
    cs = jnp.clip(qc - NA_COLS // 2, 0, GRID_W - NA_COLS)
    ok = (kc >= cs) & (kc < cs + NA_COLS)
    tab = jnp.where(ok[None, None], tab, NEG_INF)
    tab = tab.reshape(B_HEADS // 2, 2, NA_ROWS, GRID_W, keys)
    return jnp.transpose(tab, (0, 2, 1, 3, 4)).reshape(B_HEADS // 2, NA_ROWS, 2 * GRID_W, keys)


def _nbr_attn(qb, kb, vb, bias_tab):
    b, seq_len, _ = qb.shape
    n_rows = seq_len // GRID_W
    assert n_rows >= NA_ROWS and n_rows % NA_QROWS == 0
    qtile = NA_QROWS * GRID_W
    whole = pl.BlockSpec((None, seq_len, B_QKV), lambda s, m: (s, 0, 0))
    return pl.pallas_call(
        functools.partial(_nbr_attn_kernel, n_rows=n_rows),
        out_shape=jax.ShapeDtypeStruct((b, seq_len, B_QKV), BF16),
        grid=(b, n_rows // NA_QROWS),
        in_specs=[
            pl.BlockSpec((None, qtile, B_QKV), lambda s, m: (s, m, 0)),
            whole, whole,
            _const_spec((B_HEADS // 2, NA_ROWS, 2 * GRID_W, NA_ROWS * GRID_W)),
        ],
        out_specs=pl.BlockSpec((None, qtile, B_QKV), lambda s, m: (s, m, 0)),
        compiler_params=_params("parallel", "arbitrary"),
        name="nbr_attn",
    )(qb, kb, vb, bias_tab)


def _mix_out_kernel(x_ref, ya_ref, yb_ref, w_ref, o_ref):
    half = ya_ref.shape[-1]
    o_ref[...] = (x_ref[...]
                  + _dot(ya_ref[...], w_ref[:half, :])
                  + _dot(yb_ref[...], w_ref[half:, :]))


def _attn_out(x2d, ya2d, yb2d, w):
    t = x2d.shape[0]
    return pl.pallas_call(
        _mix_out_kernel,
        out_shape=jax.ShapeDtypeStruct(x2d.shape, F32),
        grid=(t // TOKEN_TILE,),
        in_specs=[
            pl.BlockSpec((TOKEN_TILE, D_MODEL), lambda i: (i, 0)),
            pl.BlockSpec((TOKEN_TILE, A_Q), lambda i: (i, 0)),
            pl.BlockSpec((TOKEN_TILE, B_QKV), lambda i: (i, 0)),
            _const_spec((A_Q + B_QKV, D_MODEL)),
        ],
        out_specs=pl.BlockSpec((TOKEN_TILE, D_MODEL), lambda i: (i, 0)),
        compiler_params=_params("parallel"),
        name="attn_out",
    )(x2d, ya2d, yb2d, w)


def _ssm_in_kernel(x_ref, g_ref, w_ref, uc_ref, adgd_ref):
    h = _dot(_rms(x_ref[...], g_ref[...]).astype(BF16), w_ref[...])
    uc_ref[...] = h[:, :C_WIDTH]
    adgd_ref[...] = h[:, C_WIDTH:]


def _ssm_in(x2d, g, w):
    t = x2d.shape[0]
    widths = (C_WIDTH, 2 * D_WIDTH)
    return pl.pallas_call(
        _ssm_in_kernel,
        out_shape=[jax.ShapeDtypeStruct((t, n), F32) for n in widths],
        grid=(t // TOKEN_TILE,),
        in_specs=[
            pl.BlockSpec((TOKEN_TILE, D_MODEL), lambda i: (i, 0)),
            _const_spec((1, D_MODEL)),
            _const_spec((D_MODEL, C_WIDTH + 2 * D_WIDTH)),
        ],
        out_specs=[pl.BlockSpec((TOKEN_TILE, n), lambda i: (i, 0)) for n in widths],
        compiler_params=_params("parallel"),
        name="ssm_in",
    )(x2d, g, w)


SLAB_GROUPS = LANES // SSM_GROUP_CH
N_SLABS = SSM_GROUPS // SLAB_GROUPS
SLAB_HALF = SLAB_GROUPS * SSM_STATE
SLAB = 2 * SLAB_HALF
STATE_W = N_SLABS * SLAB


def _ssm_scan_kernel(uf_ref, ub_ref, b_ref, a_ref, c_ref, yf_ref, yb_ref, s_ref, h_ref):
    c = pl.program_id(1)
    rows = SCAN_CHUNK * SCAN_SEQS

    @pl.when(c == 0)
    def _():
        h_ref[...] = jnp.zeros_like(h_ref)

    for d, u_ref in enumerate((uf_ref, ub_ref)):
        u2d = u_ref[...].reshape(rows, C_WIDTH).astype(BF16)
        for j in range(N_SLABS):
            bu = _dot(u2d[:, j * LANES:(j + 1) * LANES], b_ref[d, j])
            s_ref[d, :, :, j * SLAB:(j + 1) * SLAB] = bu.reshape(SCAN_CHUNK, SCAN_SEQS, SLAB)

    for pair in range(N_SLABS // 2):
        slabs = (2 * pair, 2 * pair + 1)

        def step(i, carry):
            new = []
            for d in range(2):
                t = i if d == 0 else SCAN_CHUNK - 1 - i
                for idx, j in enumerate(slabs):
                    hr, hi = carry[2 * (2 * d + idx)], carry[2 * (2 * d + idx) + 1]
                    re = slice(j * SLAB, j * SLAB + SLAB_HALF)
                    im = slice(j * SLAB + SLAB_HALF, (j + 1) * SLAB)
                    ar = a_ref[d, :, re]
                    ai = a_ref[d, :, im]
                    nr = ar * hr - ai * hi + s_ref[d, t, :, re]
                    ni = ar * hi + ai * hr + s_ref[d, t, :, im]
                    s_ref[d, t, :, re] = nr
                    s_ref[d, t, :, im] = ni
                    new += [nr, ni]
            return tuple(new)

        init = []
        for d in range(2):
            for j in slabs:
                init += [h_ref[d, :, j * SLAB:j * SLAB + SLAB_HALF],
                         h_ref[d, :, j * SLAB + SLAB_HALF:(j + 1) * SLAB]]
        final = lax.fori_loop(0, SCAN_CHUNK, step, tuple(init))
        k = 0
        for d in range(2):
            for j in slabs:
                h_ref[d, :, j * SLAB:j * SLAB + SLAB_HALF] = final[k]
                h_ref[d, :, j * SLAB + SLAB_HALF:(j + 1) * SLAB] = final[k + 1]
                k += 2

    for d, y_ref in enumerate((yf_ref, yb_ref)):
        for j in range(N_SLABS):
            hs = s_ref[d, :, :, j * SLAB:(j + 1) * SLAB].reshape(rows, SLAB).astype(BF16)
            y = _dot(hs, c_ref[d, j])
            y_ref[:, :, j * LANES:(j + 1) * LANES] = y.reshape(SCAN_CHUNK, SCAN_SEQS, LANES)


def _ssm_scan_weights(lam_re, lam_im, log_dt, b_re, b_im, c_re, c_im):
    lam = lax.complex(lam_re.astype(F32), lam_im.astype(F32))
    dt = jnp.exp(log_dt.astype(F32))[:, :, None]
    lam_bar = jnp.exp(lam * dt)
    b_bar = ((lam_bar - 1.0) / lam)[..., None] * lax.complex(b_re.astype(F32), b_im.astype(F32))
    eye = jnp.eye(SLAB_GROUPS, dtype=F32)

    def pack_b(part):
        x = part.reshape(2, N_SLABS, SLAB_GROUPS, SSM_STATE, SSM_GROUP_CH)
        x = jnp.einsum('dsgnp,gh->dsgphn', x, eye)
        return x.reshape(2, N_SLABS, LANES, SLAB_HALF)

    b_pack = jnp.concatenate([pack_b(jnp.real(b_bar)), pack_b(jnp.imag(b_bar))], axis=-1).astype(BF16)

    def pack_a(part):
        return part.reshape(2, N_SLABS, SLAB_HALF)

    a_pack = jnp.concatenate([pack_a(jnp.real(lam_bar)), pack_a(jnp.imag(lam_bar))], axis=-1)
    a_pack = jnp.broadcast_to(a_pack.reshape(2, 1, STATE_W), (2, SCAN_SEQS, STATE_W))

    def pack_c(part):
        x = part.astype(F32).reshape(2, N_SLABS, SLAB_GROUPS, SSM_GROUP_CH, SSM_STATE)
        x = jnp.einsum('dsgpn,gh->dsgnhp', x, eye)
        return x.reshape(2, N_SLABS, SLAB_HALF, LANES)

    c_pack = jnp.concatenate([pack_c(c_re), -pack_c(c_im)], axis=2).astype(BF16)
    return b_pack, a_pack, c_pack


def _ssm_scan(u_tm, b_pack, a_pack, c_pack):
    seq_len, b, _ = u_tm.shape
    nc = seq_len // SCAN_CHUNK
    blk = (SCAN_CHUNK, SCAN_SEQS, C_WIDTH)
    return pl.pallas_call(
        _ssm_scan_kernel,
        out_shape=[jax.ShapeDtypeStruct(u_tm.shape, F32)] * 2,
        grid=(b // SCAN_SEQS, nc),
        in_specs=[
            pl.BlockSpec(blk, lambda s, c: (c, s, 0)),
            pl.BlockSpec(blk, lambda s, c: (nc - 1 - c, s, 0)),
            _const_spec((2, N_SLABS, LANES, SLAB)),
            _const_spec((2, SCAN_SEQS, STATE_W)),
            _const_spec((2, N_SLABS, SLAB, LANES)),
        ],
        out_specs=[
            pl.BlockSpec(blk, lambda s, c: (c, s, 0)),
            pl.BlockSpec(blk, lambda s, c: (nc - 1 - c, s, 0)),
        ],
        scratch_shapes=[pltpu.VMEM((2, SCAN_CHUNK, SCAN_SEQS, STATE_W), F32),
                        pltpu.VMEM((2, SCAN_SEQS, STATE_W), F32)],
        compiler_params=_params("parallel", "arbitrary"),
        name="ssm_scan",
    )(u_tm, u_tm, b_pack, a_pack, c_pack)


def _gelu_tanh(x):
    return 0.5 * x * (1.0 + jnp.tanh(math.sqrt(2.0 / math.pi) * (x + 0.044715 * (x * x * x))))


def _ssm_out_kernel(x_ref, yf_ref, yb_ref, uc_ref, cur_ref, prev_ref, next_ref,
                    dskip_ref, wglu_ref, bglu_ref, cw_ref, cb_ref, lng_ref, lnb_ref, w_ref,
                    o_ref, pad_ref, yd_ref):
    i = pl.program_id(1)
    last = pl.num_programs(1) - 1
    tile = cur_ref.shape[0]

    z = _gelu_tanh(yf_ref[...] + yb_ref[...] + dskip_ref[...] * uc_ref[...])
    yc = z * jax.nn.sigmoid(_dot(z.astype(BF16), wglu_ref[...]) + bglu_ref[...])

    def glu(ref):
        v = ref[...]
        return v[:, :D_WIDTH] * jax.nn.sigmoid(v[:, D_WIDTH:])

    pad_ref[0:CONV_HALO, :] = jnp.where(i > 0, glu(prev_ref), 0.0)
    pad_ref[CONV_HALO:CONV_HALO + tile, :] = glu(cur_ref)
    pad_ref[CONV_HALO + tile:, :] = jnp.where(i < last, glu(next_ref), 0.0)
    shift = CONV_HALO - CONV_WIDTH // 2
    span = CONV_ROWS + SUBLANES
    for r0 in range(0, tile, CONV_ROWS):
        for lt in range(D_WIDTH // LANES):
            lanes = slice(lt * LANES, (lt + 1) * LANES)
            acc = None
            for b in range(SUBLANES):
                part = None
                for a in range(-(-(CONV_WIDTH + shift) // SUBLANES)):
                    k = SUBLANES * a + b - shift
                    if 0 <= k < CONV_WIDTH:
                        term = cw_ref[k:k + 1, lanes] * pad_ref[r0 + SUBLANES * a:r0 + SUBLANES * a + span, lanes]
                        part = term if part is None else part + term
                part = part[b:b + CONV_ROWS, :]
                acc = part if acc is None else acc + part
            yd_ref[r0:r0 + CONV_ROWS, lanes] = acc + cb_ref[:, lanes]
        conv = yd_ref[r0:r0 + CONV_ROWS, :]
        mu = jnp.mean(conv, axis=-1, keepdims=True)
        xc = conv - mu
        y = xc * lax.rsqrt(jnp.mean(xc * xc, axis=-1, keepdims=True) + EPS) * lng_ref[...] + lnb_ref[...]
        yd_ref[r0:r0 + CONV_ROWS, :] = y * jax.nn.sigmoid(y)

    o_ref[...] = (x_ref[...]
                  + _dot(yc.astype(BF16), w_ref[:C_WIDTH, :])
                  + _dot(yd_ref[...].astype(BF16), w_ref[C_WIDTH:, :]))


def _ssm_out(x, yf, yb, uc, adgd, d_skip, w_glu, b_glu, conv_w, conv_b, ln_g, ln_b, w_out):
    b, seq_len, _ = x.shape
    tile = TOKEN_TILE
    per = tile // CONV_HALO
    n_halo = seq_len // CONV_HALO

    def tok(width):
        return pl.BlockSpec((None, tile, width), lambda s, i: (s, i, 0))

    return pl.pallas_call(
        _ssm_out_kernel,
        out_shape=jax.ShapeDtypeStruct(x.shape, F32),
        grid=(b, seq_len // tile),
        in_specs=[
            tok(D_MODEL), tok(C_WIDTH), tok(C_WIDTH), tok(C_WIDTH), tok(2 * D_WIDTH),
            pl.BlockSpec((None, CONV_HALO, 2 * D_WIDTH), lambda s, i: (s, jnp.maximum(i * per - 1, 0), 0)),
            pl.BlockSpec((None, CONV_HALO, 2 * D_WIDTH),
                         lambda s, i: (s, jnp.minimum((i + 1) * per, n_halo - 1), 0)),
            _const_spec((1, C_WIDTH)),
            _const_spec((C_WIDTH, C_WIDTH)),
            _const_spec((1, C_WIDTH)),
            _const_spec((CONV_WIDTH, D_WIDTH)),
            _const_spec((1, D_WIDTH)),
            _const_spec((1, D_WIDTH)),
            _const_spec((1, D_WIDTH)),
            _const_spec((C_WIDTH + D_WIDTH, D_MODEL)),
        ],
        out_specs=tok(D_MODEL),
        scratch_shapes=[pltpu.VMEM((tile + 2 * CONV_HALO, D_WIDTH), F32),
                        pltpu.VMEM((tile, D_WIDTH), F32)],
        compiler_params=_params("parallel", "arbitrary"),
        name="ssm_out",
    )(x, yf, yb, uc, adgd, adgd, adgd, d_skip, w_glu, b_glu, conv_w, conv_b, ln_g, ln_b, w_out)


def _row(v):
    return v.astype(F32).reshape(1, -1)


def _attn_layer(x, w, j):
    b, seq_len, _ = x.shape
    x2d = x.reshape(b * seq_len, D_MODEL)
    w_in, gain = w['attn_in'][j]
    qa, kva, qb, kb, vb = _attn_in(x2d, w['norm_mix_row'], w_in, gain)
    as3d = lambda a: a.reshape(b, seq_len, a.shape[-1])
    ya = _win_gqa(as3d(qa), as3d(kva), w['sink_a'][j].astype(F32))
    yb = _nbr_attn(as3d(qb), as3d(kb), as3d(vb), w['nbr_bias'][j])
    out = _attn_out(x2d, ya.reshape(b * seq_len, A_Q), yb.reshape(b * seq_len, B_QKV), w['w_attn_out'][j])
    return out.reshape(x.shape)


def _ssm_layer(x, w, j):
    b, seq_len, _ = x.shape
    x2d = x.reshape(b * seq_len, D_MODEL)
    uc, adgd = _ssm_in(x2d, w['norm_mix_row'], w['w_ssm_in'][j])
    uc = uc.reshape(b, seq_len, C_WIDTH)
    b_pack, a_pack, c_pack = w['scan'][j]
    yf, yb = _ssm_scan(jnp.transpose(uc, (1, 0, 2)), b_pack, a_pack, c_pack)
    yf = jnp.transpose(yf, (1, 0, 2))
    yb = jnp.transpose(yb, (1, 0, 2))
    return _ssm_out(x, yf, yb, uc, adgd.reshape(b, seq_len, 2 * D_WIDTH),
                    _row(w['d_skip'][j]), w['w_glu_c'][j], _row(w['b_glu_c'][j]),
                    w['conv_w'][j].astype(F32), _row(w['conv_b'][j]), _row(w['ln_g_d'][j]),
                    _row(w['ln_b_d'][j]), w['w_ssm_out'][j])


def _trunk(x, p, w):
    b, seq_len, _ = x.shape
    assert seq_len % TOKEN_TILE == 0 and seq_len % SCAN_CHUNK == 0 and b % SCAN_SEQS == 0
    assert seq_len >= 3 * WIN and seq_len % WIN == 0
    p3d = p.reshape(DEPTH, b * seq_len, PLE_DIM)
    for i in range(DEPTH):
        x2d = x.reshape(b * seq_len, D_MODEL)
        x2d = _ffn(x2d, _row(w['norm_ffn1'][i]), w['w_ffn1_in'][i], w['w_ffn1_out'][i])
        x = x2d.reshape(b, seq_len, D_MODEL)
        lw = dict(w, norm_mix_row=_row(w['norm_mix'][i]))
        x = _attn_layer(x, lw, i // 2) if i % 2 == 0 else _ssm_layer(x, lw, i // 2)
        x2d = x.reshape(b * seq_len, D_MODEL)
        x2d = _ffn(x2d, _row(w['norm_ffn2'][i]), w['w_ffn2_in'][i], w['w_ffn2_out'][i])
        x2d = _ple(x2d, p3d, i, _row(w['norm_ple'][i]), w['w_ple_gate'][i], w['w_ple_proj'][i],
                   _row(w['norm_ple_post'][i]))
        x = x2d.reshape(b, seq_len, D_MODEL)
    return x


def kernel(x_prompt, x_sample, p_prompt, p_sample, norm_ffn1, w_ffn1_in, w_ffn1_out, norm_mix, norm_ffn2, w_ffn2_in, w_ffn2_out, norm_ple, w_ple_gate, w_ple_proj, norm_ple_post, w_attn_in, q_gain_a, k_gain_a, sink_a, q_gain_b, k_gain_b, rpb_b, w_attn_out, w_ssm_in, lam_re, lam_im, log_dt, b_re, b_im, c_re, c_im, d_skip, w_glu_c, b_glu_c, conv_w, conv_b, ln_g_d, ln_b_d, w_ssm_out):
    w = dict(norm_ffn1=norm_ffn1, norm_mix=norm_mix, norm_ffn2=norm_ffn2, norm_ple=norm_ple,
             norm_ple_post=norm_ple_post, q_gain_a=q_gain_a, k_gain_a=k_gain_a, sink_a=sink_a,
             q_gain_b=q_gain_b, k_gain_b=k_gain_b, d_skip=d_skip, b_glu_c=b_glu_c, conv_w=conv_w,
             conv_b=conv_b, ln_g_d=ln_g_d, ln_b_d=ln_b_d)
    for name, val in dict(w_ffn1_in=w_ffn1_in, w_ffn1_out=w_ffn1_out, w_ffn2_in=w_ffn2_in,
                          w_ffn2_out=w_ffn2_out, w_ple_gate=w_ple_gate, w_ple_proj=w_ple_proj,
                          w_attn_out=w_attn_out, w_ssm_in=w_ssm_in,
                          w_glu_c=w_glu_c, w_ssm_out=w_ssm_out).items():
        w[name] = val.astype(BF16)
    w['nbr_bias'] = [_nbr_bias_table(rpb_b[j]) for j in range(rpb_b.shape[0])]
    w['attn_in'] = [_attn_in_weights(w_attn_in[j], q_gain_a[j], k_gain_a[j], q_gain_b[j], k_gain_b[j])
                    for j in range(w_attn_in.shape[0])]
    w['scan'] = [_ssm_scan_weights(lam_re[j], lam_im[j], log_dt[j], b_re[j], b_im[j], c_re[j], c_im[j])
                 for j in range(lam_re.shape[0])]
    return (_trunk(x_prompt, p_prompt, w), _trunk(x_sample, p_sample, w))
```

```python
import functools
import math

import jax
import jax.numpy as jnp
from jax import lax
from jax.experimental import pallas as pl
from jax.experimental.pallas import tpu as pltpu

D_MODEL = 1024
DEPTH = 4
HEAD_DIM = 64
A_HEADS = 8
A_KV_HEADS = 2
A_GROUP = A_HEADS // A_KV_HEADS
WIN = 128
B_HEADS = 8
GRID_W = 64
NA_ROWS = 8
NA_COLS = 16
C_WIDTH = 512
SSM_GROUP_CH = 16
SSM_GROUPS = C_WIDTH // SSM_GROUP_CH
SSM_STATE = 64
D_WIDTH = 512
CONV_WIDTH = 31
D_FF = 2816
PLE_DIM = 256
A_Q = A_HEADS * HEAD_DIM
A_KV = A_KV_HEADS * HEAD_DIM
B_QKV = B_HEADS * HEAD_DIM
ATTN_IN = A_Q + 2 * A_KV + 3 * B_QKV
NEG_INF = -1e30
EPS = 1e-6
LOG2E = math.log2(math.e)

BF16 = jnp.bfloat16
F32 = jnp.float32

VMEM_LIMIT_BYTES = 52 * 1024 * 1024
LANES = 128
SUBLANES = 8
TOKEN_TILE = 512
FF_CHUNK = 256
SCAN_CHUNK = 64
SCAN_SEQS = SUBLANES
NA_QROWS = 8
CONV_HALO = 16
CONV_ROWS = 64


def _params(*sem):
    return pltpu.CompilerParams(dimension_semantics=sem, vmem_limit_bytes=VMEM_LIMIT_BYTES)


def _rms(x, g):
    return x * lax.rsqrt(jnp.mean(x * x, axis=-1, keepdims=True) + EPS) * g


def _dot(a, b):
    return jnp.dot(a, b, preferred_element_type=F32)


def _dot_nt(a, b):
    return lax.dot_general(a, b, (((1,), (1,)), ((), ())), preferred_element_type=F32)


def _const_spec(shape):
    nd = len(shape)
    return pl.BlockSpec(shape, lambda *_: (0,) * nd, pipeline_mode=pl.Buffered(1))


def _ffn_body(x, g_ref, win_ref, wout_ref, acc_ref):
    xn = _rms(x, g_ref[...]).astype(BF16)
    for c in range(D_FF // FF_CHUNK):
        lo = c * FF_CHUNK
        gate = _dot(xn, win_ref[:, lo:lo + FF_CHUNK])
        up = _dot(xn, win_ref[:, D_FF + lo:D_FF + lo + FF_CHUNK])
        act = (gate * jax.nn.sigmoid(gate) * up).astype(BF16)
        part = _dot(act, wout_ref[lo:lo + FF_CHUNK, :])
        if c == 0:
            acc_ref[...] = part
        else:
            acc_ref[...] += part
    return x + 0.5 * acc_ref[...]


def _ple_body(x, p_ref, g1_ref, wg_ref, wp_ref, g2_ref):
    gate = jax.nn.sigmoid(_dot(_rms(x, g1_ref[...]).astype(BF16), wg_ref[...]))
    proj = _dot(p_ref[...].astype(BF16), wp_ref[...])
    return x + gate * _rms(proj, g2_ref[...])


def _tok_spec(width):
    return pl.BlockSpec((None, TOKEN_TILE, width), lambda s, i: (s, i, 0))


def _ffn_specs():
    return [_const_spec((1, D_MODEL)), _const_spec((D_MODEL, 2 * D_FF)), _const_spec((D_FF, D_MODEL))]


def _ple_specs():
    return [_const_spec((1, D_MODEL)), _const_spec((D_MODEL, D_MODEL)), _const_spec((PLE_DIM, D_MODEL)),
            _const_spec((1, D_MODEL))]


def _post_kernel(*refs, n_mix):
    x_ref = refs[0]
    mix_refs = refs[1:1 + n_mix]
    ffn_refs = refs[1 + n_mix:4 + n_mix]
    p_ref = refs[4 + n_mix]
    ple_refs = refs[5 + n_mix:9 + n_mix]
    o_ref, acc_ref = refs[9 + n_mix:]
    x = x_ref[...]
    if n_mix:
        ya_ref, yb_ref, w_ref = mix_refs
        half = ya_ref.shape[-1]
        x = x + _dot(ya_ref[...], w_ref[:half, :]) + _dot(yb_ref[...], w_ref[half:, :])
    x = _ffn_body(x, *ffn_refs, acc_ref)
    o_ref[...] = _ple_body(x, p_ref, *ple_refs)


def _post(x, mix, ffn, p, layer, ple):
    b, seq_len, _ = x.shape
    mix_specs = [_tok_spec(mix[0].shape[-1]), _tok_spec(mix[1].shape[-1]), _const_spec(mix[2].shape)] if mix else []
    return pl.pallas_call(
        functools.partial(_post_kernel, n_mix=len(mix)),
        out_shape=jax.ShapeDtypeStruct(x.shape, F32),
        grid=(b, seq_len // TOKEN_TILE),
        in_specs=([_tok_spec(D_MODEL)] + mix_specs + _ffn_specs()
                  + [pl.BlockSpec((None, None, TOKEN_TILE, PLE_DIM), lambda s, i: (layer, s, i, 0))] + _ple_specs()),
        out_specs=_tok_spec(D_MODEL),
        scratch_shapes=[pltpu.VMEM((TOKEN_TILE, D_MODEL), F32)],
        compiler_params=_params("parallel", "parallel"),
        name="post",
    )(x, *mix, *ffn, p, *ple)


ATTN_CHUNK = 512
ATTN_COLS = 5 * ATTN_CHUNK
ATTN_NORM_PIECES = (0, 1, 2, 4, 5, 6, 7)
NORM_PIECE = 256


def _pre_attn_kernel(x_ref, g1_ref, win_ref, wout_ref, g_ref, w_ref, gain_ref, ones_ref,
                     x1_ref, qa_ref, kva_ref, qb_ref, kb_ref, vb_ref, acc_ref):
    x1 = _ffn_body(x_ref[...], g1_ref, win_ref, wout_ref, acc_ref)
    x1_ref[...] = x1
    xn = _rms(x1, g_ref[...]).astype(BF16)
    for c, ref in enumerate((qa_ref, kva_ref, qb_ref, kb_ref, vb_ref)):
        h = _dot(xn, w_ref[:, c * ATTN_CHUNK:(c + 1) * ATTN_CHUNK])
        for half in range(ATTN_CHUNK // NORM_PIECE):
            piece = c * (ATTN_CHUNK // NORM_PIECE) + half
            lanes = slice(half * NORM_PIECE, (half + 1) * NORM_PIECE)
            hp = h[:, lanes]
            if piece in ATTN_NORM_PIECES:
                ss = _dot((hp * hp).astype(BF16), ones_ref[...])
                hp = hp * lax.rsqrt(ss * (1.0 / HEAD_DIM) + EPS) * gain_ref[:, piece * NORM_PIECE:(piece + 1) * NORM_PIECE]
            ref[:, lanes] = hp.astype(BF16)


def _attn_in_weights(w, q_gain_a, k_gain_a, q_gain_b, k_gain_b):
    ka = [w[:, A_Q + h * HEAD_DIM:A_Q + (h + 1) * HEAD_DIM] for h in range(A_KV_HEADS)]
    va = [w[:, A_Q + A_KV + h * HEAD_DIM:A_Q + A_KV + (h + 1) * HEAD_DIM] for h in range(A_KV_HEADS)]
    w2 = jnp.concatenate([w[:, :A_Q]] + [c for h in range(A_KV_HEADS) for c in (ka[h], ka[h])]
                         + [c for h in range(A_KV_HEADS) for c in (va[h], va[h])]
                         + [w[:, A_Q + 2 * A_KV:]], axis=1).astype(BF16)
    scale = HEAD_DIM ** -0.5 * LOG2E
    f = lambda g, reps, s=1.0: jnp.tile(g.astype(F32) * s, reps)
    gain = jnp.concatenate([f(q_gain_a, A_HEADS, scale), f(k_gain_a, 2 * A_KV_HEADS), jnp.ones((2 * A_KV,), F32),
                            f(q_gain_b, B_HEADS, scale), f(k_gain_b, B_HEADS), jnp.ones((B_QKV,), F32)])
    return w2, gain.reshape(1, ATTN_COLS)


def _head_ones():
    idx = jnp.arange(NORM_PIECE) // HEAD_DIM
    return (idx[:, None] == idx[None, :]).astype(BF16)


def _pre_attn(x, ffn, g, w2, gain):
    b, seq_len, _ = x.shape
    return pl.pallas_call(
        _pre_attn_kernel,
        out_shape=[jax.ShapeDtypeStruct(x.shape, F32)] + [jax.ShapeDtypeStruct((b, seq_len, ATTN_CHUNK), BF16)] * 5,
        grid=(b, seq_len // TOKEN_TILE),
        in_specs=[_tok_spec(D_MODEL)] + _ffn_specs() + [
            _const_spec((1, D_MODEL)),
            _const_spec((D_MODEL, ATTN_COLS)),
            _const_spec((1, ATTN_COLS)),
            _const_spec((NORM_PIECE, NORM_PIECE)),
        ],
        out_specs=[_tok_spec(D_MODEL)] + [_tok_spec(ATTN_CHUNK)] * 5,
        scratch_shapes=[pltpu.VMEM((TOKEN_TILE, D_MODEL), F32)],
        compiler_params=_params("parallel", "parallel"),
        name="pre_attn",
    )(x, *ffn, g, w2, gain, _head_ones())


def _pair_rows(q_tile):
    upper = lax.broadcasted_iota(jnp.int32, q_tile.shape, 1) >= HEAD_DIM
    zero = jnp.zeros_like(q_tile)
    return jnp.concatenate([jnp.where(upper, zero, q_tile), jnp.where(upper, q_tile, zero)], axis=0)


def _unpair_rows(o2):
    r = o2.shape[0] // 2
    upper = lax.broadcasted_iota(jnp.int32, (r, LANES), 1) >= HEAD_DIM
    return jnp.where(upper, o2[r:], o2[:r])


WIN_SPAN = 3 * WIN


def _win_gqa_kernel(sink_ref, qa_ref, kva_ref, bias_ref, o_ref, *, seq_len):
    n = pl.program_id(1)
    start = pl.multiple_of(jnp.clip((n - 1) * WIN, 0, seq_len - WIN_SPAN), WIN)
    var = n - start // WIN
    second = lax.broadcasted_iota(jnp.int32, (2 * WIN, 1), 0) >= WIN
    n_pairs = A_HEADS // 2
    kv_lanes = [slice((j // (A_GROUP // 2)) * LANES, (j // (A_GROUP // 2) + 1) * LANES) for j in range(n_pairs)]
    scores = [_dot_nt(_pair_rows(qa_ref[:, j * LANES:(j + 1) * LANES]), kva_ref[pl.ds(start, WIN_SPAN), kv_lanes[j]])
              + bias_ref[var, j] for j in range(n_pairs)]
    sinks = [jnp.where(second, sink_ref[2 * j + 1], sink_ref[2 * j]) for j in range(n_pairs)]
    maxes = [jnp.maximum(jnp.max(s, axis=-1, keepdims=True), sink) for s, sink in zip(scores, sinks)]
    es = [jnp.exp2(s - m) for s, m in zip(scores, maxes)]
    o2s = [_dot(e.astype(BF16), kva_ref[pl.ds(start, WIN_SPAN), 2 * A_KV + kv_lanes[j].start:2 * A_KV + kv_lanes[j].stop])
           for j, e in enumerate(es)]
    for j in range(n_pairs):
        denom = jnp.sum(es[j], axis=-1, keepdims=True) + jnp.exp2(sinks[j] - maxes[j])
        o_ref[:, j * LANES:(j + 1) * LANES] = _unpair_rows(o2s[j] / denom).astype(BF16)


def _win_bias_table():
    off = jnp.arange(3)[:, None, None] * WIN
    dist = jnp.abs(jnp.arange(WIN)[None, :, None] + off - jnp.arange(WIN_SPAN)[None, None, :])
    slopes = jnp.exp2(-8.0 * jnp.arange(1, A_HEADS + 1, dtype=F32) / A_HEADS) * LOG2E
    bias = -slopes[None, :, None, None] * dist[:, None].astype(F32)
    bias = jnp.where((dist <= WIN)[:, None], bias, NEG_INF)
    return bias.reshape(3, A_HEADS // 2, 2 * WIN, WIN_SPAN)


def _win_gqa(qa, kva, sink, bias_tab):
    b, seq_len, _ = qa.shape
    return pl.pallas_call(
        functools.partial(_win_gqa_kernel, seq_len=seq_len),
        out_shape=jax.ShapeDtypeStruct((b, seq_len, A_Q), BF16),
        grid=(b, seq_len // WIN),
        in_specs=[
            pl.BlockSpec(memory_space=pltpu.SMEM),
            pl.BlockSpec((None, WIN, A_Q), lambda i, n: (i, n, 0)),
            pl.BlockSpec((None, seq_len, ATTN_CHUNK), lambda i, n: (i, 0, 0)),
            _const_spec((3, A_HEADS // 2, 2 * WIN, WIN_SPAN)),
        ],
        out_specs=pl.BlockSpec((None, WIN, A_Q), lambda i, n: (i, n, 0)),
        compiler_params=_params("parallel", "arbitrary"),
        name="win_gqa",
    )(sink, qa, kva, bias_tab)


def _nbr_attn_kernel(qb_ref, kb_ref, vb_ref, bias_ref, o_ref, *, n_rows):
    m = pl.program_id(1)
    keys = NA_ROWS * GRID_W

    pairs = [slice(pr * LANES, (pr + 1) * LANES) for pr in range(B_HEADS // 2)]

    def key_offset(i):
        r = NA_QROWS * m + i
        rs = jnp.clip(r - NA_ROWS // 2, 0, n_rows - NA_ROWS)
        return pl.multiple_of(rs * GRID_W, GRID_W), r - rs

    def scores(i):
        off, var = key_offset(i)
        return [_dot_nt(_pair_rows(qb_ref[i * GRID_W:(i + 1) * GRID_W, lanes]), kb_ref[pl.ds(off, keys), lanes])
                + bias_ref[pr, var] for pr, lanes in enumerate(pairs)]

    def finish(i, o2s, denoms):
        for lanes, o2, denom in zip(pairs, o2s, denoms):
            o_ref[i * GRID_W:(i + 1) * GRID_W, lanes] = _unpair_rows(o2 / denom).astype(BF16)

    s_cur, pending = scores(0), None
    for i in range(NA_QROWS):
        s_next = scores(i + 1) if i + 1 < NA_QROWS else None
        off, _ = key_offset(i)
        es = [jnp.exp2(s - jnp.max(s, axis=-1, keepdims=True)) for s in s_cur]
        o2s = [_dot(e.astype(BF16), vb_ref[pl.ds(off, keys), lanes]) for e, lanes in zip(es, pairs)]
        denoms = [jnp.sum(e, axis=-1, keepdims=True) for e in es]
        if pending is not None:
            finish(*pending)
        s_cur, pending = s_next, (i, o2s, denoms)
    finish(*pending)


def _nbr_bias_table(rpb):
    keys = NA_ROWS * GRID_W
    z = jnp.pad(rpb.astype(F32), ((0, 0), (0, 0), (0, GRID_W - (2 * NA_COLS - 1)))).reshape(B_HEADS, -1)
    z = jnp.stack([z[:, (NA_ROWS - 1 - v) * GRID_W:(NA_ROWS - 1 - v) * GRID_W + keys] for v in range(NA_ROWS)], axis=1)
    period = keys + GRID_W
    zz = jnp.concatenate([z[..., NA_COLS - 1:], jnp.zeros((B_HEADS, NA_ROWS, period - keys), F32), z[..., :NA_COLS - 1]], axis=-1)
    skew = jnp.tile(zz, (1, 1, GRID_W))[..., :GRID_W * (period - 1)].reshape(B_HEADS, NA_ROWS, GRID_W, period - 1)
    tab = skew[..., :keys]
    qc = jnp.arange(GRID_W)[:, None]
    kc = jnp.arange(keys)[None, :] % GRID_W
    cs = jnp.clip(qc - NA_COLS // 2, 0, GRID_W - NA_COLS)
    ok = (kc >= cs) & (kc < cs + NA_COLS)
    tab = jnp.where(ok[None, None], tab * LOG2E, NEG_INF)
    tab = tab.reshape(B_HEADS // 2, 2, NA_ROWS, GRID_W, keys)
    return jnp.transpose(tab, (0, 2, 1, 3, 4)).reshape(B_HEADS // 2, NA_ROWS, 2 * GRID_W, keys)


def _nbr_attn(qb, kb, vb, bias_tab):
    b, seq_len, _ = qb.shape
    n_rows = seq_len // GRID_W
    assert n_rows >= NA_ROWS and n_rows % NA_QROWS == 0
    qtile = NA_QROWS * GRID_W
    whole = pl.BlockSpec((None, seq_len, B_QKV), lambda s, m: (s, 0, 0))
    return pl.pallas_call(
        functools.partial(_nbr_attn_kernel, n_rows=n_rows),
        out_shape=jax.ShapeDtypeStruct((b, seq_len, B_QKV), BF16),
        grid=(b, n_rows // NA_QROWS),
        in_specs=[
            pl.BlockSpec((None, qtile, B_QKV), lambda s, m: (s, m, 0)),
            whole, whole,
            _const_spec((B_HEADS // 2, NA_ROWS, 2 * GRID_W, NA_ROWS * GRID_W)),
        ],
        out_specs=pl.BlockSpec((None, qtile, B_QKV), lambda s, m: (s, m, 0)),
        compiler_params=_params("parallel", "arbitrary"),
        name="nbr_attn",
    )(qb, kb, vb, bias_tab)


def _pre_ssm_kernel(x_ref, g1_ref, win_ref, wout_ref, g_ref, w_ref, x1_ref, uc_ref, adgd_ref, acc_ref):
    x1 = _ffn_body(x_ref[...], g1_ref, win_ref, wout_ref, acc_ref)
    x1_ref[...] = x1
    xn = _rms(x1, g_ref[...]).astype(BF16)
    uc_ref[...] = _dot(xn, w_ref[:, :C_WIDTH])
    adgd_ref[...] = _dot(xn, w_ref[:, C_WIDTH:])


def _pre_ssm(x, ffn, g, w):
    b, seq_len, _ = x.shape
    widths = (D_MODEL, C_WIDTH, 2 * D_WIDTH)
    return pl.pallas_call(
        _pre_ssm_kernel,
        out_shape=[jax.ShapeDtypeStruct((b, seq_len, n), F32) for n in widths],
        grid=(b, seq_len // TOKEN_TILE),
        in_specs=[_tok_spec(D_MODEL)] + _ffn_specs() + [
            _const_spec((1, D_MODEL)),
            _const_spec((D_MODEL, C_WIDTH + 2 * D_WIDTH)),
        ],
        out_specs=[_tok_spec(n) for n in widths],
        scratch_shapes=[pltpu.VMEM((TOKEN_TILE, D_MODEL), F32)],
        compiler_params=_params("parallel", "parallel"),
        name="pre_ssm",
    )(x, *ffn, g, w)


SLAB_GROUPS = LANES // SSM_GROUP_CH
N_SLABS = SSM_GROUPS // SLAB_GROUPS
SLAB_HALF = SLAB_GROUPS * SSM_STATE
SLAB = 2 * SLAB_HALF
STATE_W = N_SLABS * SLAB


def _ssm_scan_kernel(uf_ref, ub_ref, b_ref, a_ref, c_ref, yf_ref, yb_ref, s_ref, h_ref):
    c = pl.program_id(1)
    rows = SCAN_CHUNK * SCAN_SEQS

    @pl.when(c == 0)
    def _():
        h_ref[...] = jnp.zeros_like(h_ref)

    for d, u_ref in enumerate((uf_ref, ub_ref)):
        u2d = u_ref[...].reshape(rows, C_WIDTH).astype(BF16)
        for j in range(N_SLABS):
            bu = _dot(u2d[:, j * LANES:(j + 1) * LANES], b_ref[d, j])
            s_ref[d, :, :, j * SLAB:(j + 1) * SLAB] = bu.reshape(SCAN_CHUNK, SCAN_SEQS, SLAB)

    for pair in range(N_SLABS // 2):
        slabs = (2 * pair, 2 * pair + 1)

        def step(i, carry):
            new = []
            for d in range(2):
                t = i if d == 0 else SCAN_CHUNK - 1 - i
                for idx, j in enumerate(slabs):
                    hr, hi = carry[2 * (2 * d + idx)], carry[2 * (2 * d + idx) + 1]
                    re = slice(j * SLAB, j * SLAB + SLAB_HALF)
                    im = slice(j * SLAB + SLAB_HALF, (j + 1) * SLAB)
                    ar = a_ref[d, :, re]
                    ai = a_ref[d, :, im]
                    nr = ar * hr - ai * hi + s_ref[d, t, :, re]
                    ni = ar * hi + ai * hr + s_ref[d, t, :, im]
                    s_ref[d, t, :, re] = nr
                    s_ref[d, t, :, im] = ni
                    new += [nr, ni]
            return tuple(new)

        init = []
        for d in range(2):
            for j in slabs:
                init += [h_ref[d, :, j * SLAB:j * SLAB + SLAB_HALF],
                         h_ref[d, :, j * SLAB + SLAB_HALF:(j + 1) * SLAB]]
        final = lax.fori_loop(0, SCAN_CHUNK, step, tuple(init), unroll=True)
        k = 0
        for d in range(2):
            for j in slabs:
                h_ref[d, :, j * SLAB:j * SLAB + SLAB_HALF] = final[k]
                h_ref[d, :, j * SLAB + SLAB_HALF:(j + 1) * SLAB] = final[k + 1]
                k += 2

    for d, y_ref in enumerate((yf_ref, yb_ref)):
        for j in range(N_SLABS):
            hs = s_ref[d, :, :, j * SLAB:(j + 1) * SLAB].reshape(rows, SLAB).astype(BF16)
            y = _dot(hs, c_ref[d, j])
            y_ref[:, :, j * LANES:(j + 1) * LANES] = y.reshape(SCAN_CHUNK, SCAN_SEQS, LANES)


def _ssm_scan_weights(lam_re, lam_im, log_dt, b_re, b_im, c_re, c_im):
    lam = lax.complex(lam_re.astype(F32), lam_im.astype(F32))
    dt = jnp.exp(log_dt.astype(F32))[:, :, None]
    lam_bar = jnp.exp(lam * dt)
    b_bar = ((lam_bar - 1.0) / lam)[..., None] * lax.complex(b_re.astype(F32), b_im.astype(F32))
    eye = jnp.eye(SLAB_GROUPS, dtype=F32)

    def pack_b(part):
        x = part.reshape(2, N_SLABS, SLAB_GROUPS, SSM_STATE, SSM_GROUP_CH)
        x = jnp.einsum('dsgnp,gh->dsgphn', x, eye)
        return x.reshape(2, N_SLABS, LANES, SLAB_HALF)

    b_pack = jnp.concatenate([pack_b(jnp.real(b_bar)), pack_b(jnp.imag(b_bar))], axis=-1).astype(BF16)

    def pack_a(part):
        return part.reshape(2, N_SLABS, SLAB_HALF)

    a_pack = jnp.concatenate([pack_a(jnp.real(lam_bar)), pack_a(jnp.imag(lam_bar))], axis=-1)
    a_pack = jnp.broadcast_to(a_pack.reshape(2, 1, STATE_W), (2, SCAN_SEQS, STATE_W))

    def pack_c(part):
        x = part.astype(F32).reshape(2, N_SLABS, SLAB_GROUPS, SSM_GROUP_CH, SSM_STATE)
        x = jnp.einsum('dsgpn,gh->dsgnhp', x, eye)
        return x.reshape(2, N_SLABS, SLAB_HALF, LANES)

    c_pack = jnp.concatenate([pack_c(c_re), -pack_c(c_im)], axis=2).astype(BF16)
    return b_pack, a_pack, c_pack


def _ssm_scan(u_tm, b_pack, a_pack, c_pack):
    seq_len, b, _ = u_tm.shape
    nc = seq_len // SCAN_CHUNK
    blk = (SCAN_CHUNK, SCAN_SEQS, C_WIDTH)
    return pl.pallas_call(
        _ssm_scan_kernel,
        out_shape=[jax.ShapeDtypeStruct(u_tm.shape, F32)] * 2,
        grid=(b // SCAN_SEQS, nc),
        in_specs=[
            pl.BlockSpec(blk, lambda s, c: (c, s, 0)),
            pl.BlockSpec(blk, lambda s, c: (nc - 1 - c, s, 0)),
            _const_spec((2, N_SLABS, LANES, SLAB)),
            _const_spec((2, SCAN_SEQS, STATE_W)),
            _const_spec((2, N_SLABS, SLAB, LANES)),
        ],
        out_specs=[
            pl.BlockSpec(blk, lambda s, c: (c, s, 0)),
            pl.BlockSpec(blk, lambda s, c: (nc - 1 - c, s, 0)),
        ],
        scratch_shapes=[pltpu.VMEM((2, SCAN_CHUNK, SCAN_SEQS, STATE_W), F32),
                        pltpu.VMEM((2, SCAN_SEQS, STATE_W), F32)],
        compiler_params=_params("parallel", "arbitrary"),
        name="ssm_scan",
    )(u_tm, u_tm, b_pack, a_pack, c_pack)


def _gelu_tanh(x):
    return 0.5 * x * (1.0 + jnp.tanh(math.sqrt(2.0 / math.pi) * (x + 0.044715 * (x * x * x))))


def _ssm_out_kernel(x_ref, yf_ref, yb_ref, uc_ref, cur_ref, prev_ref, next_ref,
                    dskip_ref, wglu_ref, bglu_ref, cw_ref, cb_ref, lng_ref, lnb_ref, w_ref,
                    o_ref, pad_ref, yd_ref):
    i = pl.program_id(1)
    last = pl.num_programs(1) - 1
    tile = cur_ref.shape[0]

    z = _gelu_tanh(yf_ref[...] + yb_ref[...] + dskip_ref[...] * uc_ref[...])
    yc = z * jax.nn.sigmoid(_dot(z.astype(BF16), wglu_ref[...]) + bglu_ref[...])

    def glu(ref):
        v = ref[...]
        return v[:, :D_WIDTH] * jax.nn.sigmoid(v[:, D_WIDTH:])

    pad_ref[0:CONV_HALO, :] = jnp.where(i > 0, glu(prev_ref), 0.0)
    pad_ref[CONV_HALO:CONV_HALO + tile, :] = glu(cur_ref)
    pad_ref[CONV_HALO + tile:, :] = jnp.where(i < last, glu(next_ref), 0.0)
    shift = CONV_HALO - CONV_WIDTH // 2
    span = CONV_ROWS + SUBLANES
    for r0 in range(0, tile, CONV_ROWS):
        for lt in range(D_WIDTH // LANES):
            lanes = slice(lt * LANES, (lt + 1) * LANES)
            acc = None
            for b in range(SUBLANES):
                part = None
                for a in range(-(-(CONV_WIDTH + shift) // SUBLANES)):
                    k = SUBLANES * a + b - shift
                    if 0 <= k < CONV_WIDTH:
                        term = cw_ref[k:k + 1, lanes] * pad_ref[r0 + SUBLANES * a:r0 + SUBLANES * a + span, lanes]
                        part = term if part is None else part + term
                part = part[b:b + CONV_ROWS, :]
                acc = part if acc is None else acc + part
            yd_ref[r0:r0 + CONV_ROWS, lanes] = acc + cb_ref[:, lanes]
        conv = yd_ref[r0:r0 + CONV_ROWS, :]
        mu = jnp.mean(conv, axis=-1, keepdims=True)
        xc = conv - mu
        y = xc * lax.rsqrt(jnp.mean(xc * xc, axis=-1, keepdims=True) + EPS) * lng_ref[...] + lnb_ref[...]
        yd_ref[r0:r0 + CONV_ROWS, :] = y * jax.nn.sigmoid(y)

    o_ref[...] = (x_ref[...]
                  + _dot(yc.astype(BF16), w_ref[:C_WIDTH, :])
                  + _dot(yd_ref[...].astype(BF16), w_ref[C_WIDTH:, :]))


def _ssm_out(x, yf, yb, uc, adgd, d_skip, w_glu, b_glu, conv_w, conv_b, ln_g, ln_b, w_out):
    b, seq_len, _ = x.shape
    tile = TOKEN_TILE
    per = tile // CONV_HALO
    n_halo = seq_len // CONV_HALO

    def tok(width):
        return pl.BlockSpec((None, tile, width), lambda s, i: (s, i, 0))

    return pl.pallas_call(
        _ssm_out_kernel,
        out_shape=jax.ShapeDtypeStruct(x.shape, F32),
        grid=(b, seq_len // tile),
        in_specs=[
            tok(D_MODEL), tok(C_WIDTH), tok(C_WIDTH), tok(C_WIDTH), tok(2 * D_WIDTH),
            pl.BlockSpec((None, CONV_HALO, 2 * D_WIDTH), lambda s, i: (s, jnp.maximum(i * per - 1, 0), 0)),
            pl.BlockSpec((None, CONV_HALO, 2 * D_WIDTH),
                         lambda s, i: (s, jnp.minimum((i + 1) * per, n_halo - 1), 0)),
            _const_spec((1, C_WIDTH)),
            _const_spec((C_WIDTH, C_WIDTH)),
            _const_spec((1, C_WIDTH)),
            _const_spec((CONV_WIDTH, D_WIDTH)),
            _const_spec((1, D_WIDTH)),
            _const_spec((1, D_WIDTH)),
            _const_spec((1, D_WIDTH)),
            _const_spec((C_WIDTH + D_WIDTH, D_MODEL)),
        ],
        out_specs=tok(D_MODEL),
        scratch_shapes=[pltpu.VMEM((tile + 2 * CONV_HALO, D_WIDTH), F32),
                        pltpu.VMEM((tile, D_WIDTH), F32)],
        compiler_params=_params("parallel", "arbitrary"),
        name="ssm_out",
    )(x, yf, yb, uc, adgd, adgd, adgd, d_skip, w_glu, b_glu, conv_w, conv_b, ln_g, ln_b, w_out)


def _row(v):
    return v.astype(F32).reshape(1, -1)


def _attn_layer(x, ffn1, g_mix, w, j):
    w_in, gain = w['attn_in'][j]
    x1, qa, kva, qb, kb, vb = _pre_attn(x, ffn1, g_mix, w_in, gain)
    ya = _win_gqa(qa, kva, w['sink_a'][j].astype(F32) * LOG2E, w['win_bias'])
    yb = _nbr_attn(qb, kb, vb, w['nbr_bias'][j])
    return x1, (ya, yb, w['w_attn_out'][j])


def _ssm_layer(x, ffn1, g_mix, w, j):
    x1, uc, adgd = _pre_ssm(x, ffn1, g_mix, w['w_ssm_in'][j])
    b_pack, a_pack, c_pack = w['scan'][j]
    yf, yb = _ssm_scan(jnp.transpose(uc, (1, 0, 2)), b_pack, a_pack, c_pack)
    yf = jnp.transpose(yf, (1, 0, 2))
    yb = jnp.transpose(yb, (1, 0, 2))
    x2 = _ssm_out(x1, yf, yb, uc, adgd,
                  _row(w['d_skip'][j]), w['w_glu_c'][j], _row(w['b_glu_c'][j]),
                  w['conv_w'][j].astype(F32), _row(w['conv_b'][j]), _row(w['ln_g_d'][j]),
                  _row(w['ln_b_d'][j]), w['w_ssm_out'][j])
    return x2, ()


def _trunk(x, p, w):
    b, seq_len, _ = x.shape
    assert seq_len % TOKEN_TILE == 0 and seq_len % SCAN_CHUNK == 0 and b % SCAN_SEQS == 0
    assert seq_len >= WIN_SPAN and seq_len % WIN == 0
    for i in range(DEPTH):
        ffn1 = (_row(w['norm_ffn1'][i]), w['w_ffn1_in'][i], w['w_ffn1_out'][i])
        ffn2 = (_row(w['norm_ffn2'][i]), w['w_ffn2_in'][i], w['w_ffn2_out'][i])
        ple = (_row(w['norm_ple'][i]), w['w_ple_gate'][i], w['w_ple_proj'][i], _row(w['norm_ple_post'][i]))
        layer = _attn_layer if i % 2 == 0 else _ssm_layer
        x, mix = layer(x, ffn1, _row(w['norm_mix'][i]), w, i // 2)
        x = _post(x, mix, ffn2, p, i, ple)
    return x


def kernel(x_prompt, x_sample, p_prompt, p_sample, norm_ffn1, w_ffn1_in, w_ffn1_out, norm_mix, norm_ffn2, w_ffn2_in, w_ffn2_out, norm_ple, w_ple_gate, w_ple_proj, norm_ple_post, w_attn_in, q_gain_a, k_gain_a, sink_a, q_gain_b, k_gain_b, rpb_b, w_attn_out, w_ssm_in, lam_re, lam_im, log_dt, b_re, b_im, c_re, c_im, d_skip, w_glu_c, b_glu_c, conv_w, conv_b, ln_g_d, ln_b_d, w_ssm_out):
    w = dict(norm_ffn1=norm_ffn1, norm_mix=norm_mix, norm_ffn2=norm_ffn2, norm_ple=norm_ple,
             norm_ple_post=norm_ple_post, q_gain_a=q_gain_a, k_gain_a=k_gain_a, sink_a=sink_a,
             q_gain_b=q_gain_b, k_gain_b=k_gain_b, d_skip=d_skip, b_glu_c=b_glu_c, conv_w=conv_w,
             conv_b=conv_b, ln_g_d=ln_g_d, ln_b_d=ln_b_d)
    for name, val in dict(w_ffn1_in=w_ffn1_in, w_ffn1_out=w_ffn1_out, w_ffn2_in=w_ffn2_in,
                          w_ffn2_out=w_ffn2_out, w_ple_gate=w_ple_gate, w_ple_proj=w_ple_proj,
                          w_attn_out=w_attn_out, w_ssm_in=w_ssm_in,
                          w_glu_c=w_glu_c, w_ssm_out=w_ssm_out).items():
        w[name] = val.astype(BF16)
    w['nbr_bias'] = [_nbr_bias_table(rpb_b[j]) for j in range(rpb_b.shape[0])]
    w['win_bias'] = _win_bias_table()
    w['attn_in'] = [_attn_in_weights(w_attn_in[j], q_gain_a[j], k_gain_a[j], q_gain_b[j], k_gain_b[j])
                    for j in range(w_attn_in.shape[0])]
    w['scan'] = [_ssm_scan_weights(lam_re[j], lam_im[j], log_dt[j], b_re[j], b_im[j], c_re[j], c_im[j])
                 for j in range(lam_re.shape[0])]
    return (_trunk(x_prompt, p_prompt, w), _trunk(x_sample, p_sample, w))
```

```python
import functools
import math

import jax
import jax.numpy as jnp
from jax import lax
from jax.experimental import pallas as pl
from jax.experimental.pallas import tpu as pltpu

D_MODEL = 1024
DEPTH = 4
HEAD_DIM = 64
A_HEADS = 8
A_KV_HEADS = 2
A_GROUP = A_HEADS // A_KV_HEADS
WIN = 128
B_HEADS = 8
GRID_W = 64
NA_ROWS = 8
NA_COLS = 16
C_WIDTH = 512
SSM_GROUP_CH = 16
SSM_GROUPS = C_WIDTH // SSM_GROUP_CH
SSM_STATE = 64
D_WIDTH = 512
CONV_WIDTH = 31
D_FF = 2816
PLE_DIM = 256
A_Q = A_HEADS * HEAD_DIM
A_KV = A_KV_HEADS * HEAD_DIM
B_QKV = B_HEADS * HEAD_DIM
ATTN_IN = A_Q + 2 * A_KV + 3 * B_QKV
NEG_INF = -1e30
EPS = 1e-6
LOG2E = math.log2(math.e)

BF16 = jnp.bfloat16
F32 = jnp.float32

VMEM_LIMIT_BYTES = 52 * 1024 * 1024
LANES = 128
SUBLANES = 8
TOKEN_TILE = 512
FF_CHUNK = 256
SCAN_CHUNK = 64
SCAN_SEQS = SUBLANES
NA_QROWS = 8
CONV_HALO = 16
CONV_ROWS = 64


def _params(*sem):
    return pltpu.CompilerParams(dimension_semantics=sem, vmem_limit_bytes=VMEM_LIMIT_BYTES)


def _rms(x, g):
    return x * lax.rsqrt(jnp.mean(x * x, axis=-1, keepdims=True) + EPS) * g


def _dot(a, b):
    return jnp.dot(a, b, preferred_element_type=F32)


def _dot_nt(a, b):
    return lax.dot_general(a, b, (((1,), (1,)), ((), ())), preferred_element_type=F32)


def _const_spec(shape):
    nd = len(shape)
    return pl.BlockSpec(shape, lambda *_: (0,) * nd, pipeline_mode=pl.Buffered(1))


def _ffn_body(x, g_ref, win_ref, wout_ref, acc_ref, side_work=()):
    xn = _rms(x, g_ref[...]).astype(BF16)
    for c in range(D_FF // FF_CHUNK):
        lo = c * FF_CHUNK
        gate = _dot(xn, win_ref[:, lo:lo + FF_CHUNK])
        up = _dot(xn, win_ref[:, D_FF + lo:D_FF + lo + FF_CHUNK])
        act = (gate * jax.nn.sigmoid(gate) * up).astype(BF16)
        part = _dot(act, wout_ref[lo:lo + FF_CHUNK, :])
        if c == 0:
            acc_ref[...] = part
        else:
            acc_ref[...] += part
        if c < len(side_work):
            done = side_work[c](_exact_zero(part[-SUBLANES:, -LANES:]))
            acc_ref[:SUBLANES, :LANES] += _exact_zero(done)
    return x + 0.5 * acc_ref[...]


def _exact_zero(value):
    folded = value[:SUBLANES, :LANES]
    for r in range(SUBLANES, value.shape[0], SUBLANES):
        folded = folded + value[r:r + SUBLANES, :LANES]
    sixteen = jnp.uint32(16)
    bits = lax.shift_right_logical(lax.shift_right_logical(pltpu.bitcast(folded, jnp.uint32), sixteen), sixteen)
    return pltpu.bitcast(bits, F32)


def _ple_body(x, p_ref, g1_ref, wg_ref, wp_ref, g2_ref):
    gate = jax.nn.sigmoid(_dot(_rms(x, g1_ref[...]).astype(BF16), wg_ref[...]))
    proj = _dot(p_ref[...].astype(BF16), wp_ref[...])
    return x + gate * _rms(proj, g2_ref[...])


def _tok_spec(width):
    return pl.BlockSpec((None, TOKEN_TILE, width), lambda s, i: (s, i, 0))


def _ffn_specs():
    return [_const_spec((1, D_MODEL)), _const_spec((D_MODEL, 2 * D_FF)), _const_spec((D_FF, D_MODEL))]


def _ple_specs():
    return [_const_spec((1, D_MODEL)), _const_spec((D_MODEL, D_MODEL)), _const_spec((PLE_DIM, D_MODEL)),
            _const_spec((1, D_MODEL))]


def _gelu_tanh(x):
    return 0.5 * x * (1.0 + jnp.tanh(math.sqrt(2.0 / math.pi) * (x + 0.044715 * (x * x * x))))


def _post_kernel(*refs, n_mix):
    x_ref = refs[0]
    mix_refs = refs[1:1 + n_mix]
    ffn_refs = refs[1 + n_mix:4 + n_mix]
    p_ref = refs[4 + n_mix]
    ple_refs = refs[5 + n_mix:9 + n_mix]
    o_ref, acc_ref = refs[9 + n_mix:]
    if n_mix == 3:
        ya_ref, yb_ref, w_ref = mix_refs
        ya, yb = ya_ref[...], yb_ref[...]
    else:
        yf_ref, yr_ref, uc_ref, yd_ref, dskip_ref, wglu_ref, bglu_ref, w_ref = mix_refs
        z = _gelu_tanh(yf_ref[...] + yr_ref[...] + dskip_ref[...] * uc_ref[...])
        ya = (z * jax.nn.sigmoid(_dot(z.astype(BF16), wglu_ref[...]) + bglu_ref[...])).astype(BF16)
        yb = yd_ref[...]
    half = ya.shape[-1]
    x = x_ref[...] + _dot(ya, w_ref[:half, :]) + _dot(yb, w_ref[half:, :])
    x = _ffn_body(x, *ffn_refs, acc_ref)
    o_ref[...] = _ple_body(x, p_ref, *ple_refs)


def _post(x, mix, ffn, p, layer, ple):
    b, seq_len, _ = x.shape
    mix_specs = [_tok_spec(m.shape[-1]) if m.ndim == 3 else _const_spec(m.shape) for m in mix]
    return pl.pallas_call(
        functools.partial(_post_kernel, n_mix=len(mix)),
        out_shape=jax.ShapeDtypeStruct(x.shape, F32),
        grid=(b, seq_len // TOKEN_TILE),
        in_specs=([_tok_spec(D_MODEL)] + mix_specs + _ffn_specs()
                  + [pl.BlockSpec((None, None, TOKEN_TILE, PLE_DIM), lambda s, i: (layer, s, i, 0))] + _ple_specs()),
        out_specs=_tok_spec(D_MODEL),
        scratch_shapes=[pltpu.VMEM((TOKEN_TILE, D_MODEL), F32)],
        compiler_params=_params("parallel", "parallel"),
        name="post",
    )(x, *mix, *ffn, p, *ple)


ATTN_CHUNK = 512
ATTN_COLS = 5 * ATTN_CHUNK
ATTN_NORM_PIECES = (0, 1, 2, 4, 5, 6, 7)
NORM_PIECE = 256


def _pre_attn_kernel(x_ref, g1_ref, win_ref, wout_ref, g_ref, w_ref, gain_ref, ones_ref,
                     x1_ref, qa_ref, kva_ref, qb_ref, kb_ref, vb_ref, acc_ref):
    x1 = _ffn_body(x_ref[...], g1_ref, win_ref, wout_ref, acc_ref)
    x1_ref[...] = x1
    xn = _rms(x1, g_ref[...]).astype(BF16)
    for c, ref in enumerate((qa_ref, kva_ref, qb_ref, kb_ref, vb_ref)):
        h = _dot(xn, w_ref[:, c * ATTN_CHUNK:(c + 1) * ATTN_CHUNK])
        for half in range(ATTN_CHUNK // NORM_PIECE):
            piece = c * (ATTN_CHUNK // NORM_PIECE) + half
            lanes = slice(half * NORM_PIECE, (half + 1) * NORM_PIECE)
            hp = h[:, lanes]
            if piece in ATTN_NORM_PIECES:
                ss = _dot((hp * hp).astype(BF16), ones_ref[...])
                hp = hp * lax.rsqrt(ss * (1.0 / HEAD_DIM) + EPS) * gain_ref[:, piece * NORM_PIECE:(piece + 1) * NORM_PIECE]
            ref[:, lanes] = hp.astype(BF16)


def _attn_in_weights(w, q_gain_a, k_gain_a, q_gain_b, k_gain_b):
    ka = [w[:, A_Q + h * HEAD_DIM:A_Q + (h + 1) * HEAD_DIM] for h in range(A_KV_HEADS)]
    va = [w[:, A_Q + A_KV + h * HEAD_DIM:A_Q + A_KV + (h + 1) * HEAD_DIM] for h in range(A_KV_HEADS)]
    w2 = jnp.concatenate([w[:, :A_Q]] + [c for h in range(A_KV_HEADS) for c in (ka[h], ka[h])]
                         + [c for h in range(A_KV_HEADS) for c in (va[h], va[h])]
                         + [w[:, A_Q + 2 * A_KV:]], axis=1).astype(BF16)
    scale = HEAD_DIM ** -0.5 * LOG2E
    f = lambda g, reps, s=1.0: jnp.tile(g.astype(F32) * s, reps)
    gain = jnp.concatenate([f(q_gain_a, A_HEADS, scale), f(k_gain_a, 2 * A_KV_HEADS), jnp.ones((2 * A_KV,), F32),
                            f(q_gain_b, B_HEADS, scale), f(k_gain_b, B_HEADS), jnp.ones((B_QKV,), F32)])
    return w2, gain.reshape(1, ATTN_COLS)


def _head_ones():
    idx = jnp.arange(NORM_PIECE) // HEAD_DIM
    return (idx[:, None] == idx[None, :]).astype(BF16)


def _pre_attn(x, ffn, g, w2, gain):
    b, seq_len, _ = x.shape
    return pl.pallas_call(
        _pre_attn_kernel,
        out_shape=[jax.ShapeDtypeStruct(x.shape, F32)] + [jax.ShapeDtypeStruct((b, seq_len, ATTN_CHUNK), BF16)] * 5,
        grid=(b, seq_len // TOKEN_TILE),
        in_specs=[_tok_spec(D_MODEL)] + _ffn_specs() + [
            _const_spec((1, D_MODEL)),
            _const_spec((D_MODEL, ATTN_COLS)),
            _const_spec((1, ATTN_COLS)),
            _const_spec((NORM_PIECE, NORM_PIECE)),
        ],
        out_specs=[_tok_spec(D_MODEL)] + [_tok_spec(ATTN_CHUNK)] * 5,
        scratch_shapes=[pltpu.VMEM((TOKEN_TILE, D_MODEL), F32)],
        compiler_params=_params("parallel", "parallel"),
        name="pre_attn",
    )(x, *ffn, g, w2, gain, _head_ones())


def _pair_rows(q_tile):
    upper = lax.broadcasted_iota(jnp.int32, q_tile.shape, 1) >= HEAD_DIM
    zero = jnp.zeros_like(q_tile)
    return jnp.concatenate([jnp.where(upper, zero, q_tile), jnp.where(upper, q_tile, zero)], axis=0)


def _unpair_rows(o2):
    r = o2.shape[0] // 2
    upper = lax.broadcasted_iota(jnp.int32, (r, LANES), 1) >= HEAD_DIM
    return jnp.where(upper, o2[r:], o2[:r])


WIN_SPAN = 3 * WIN


def _win_gqa_kernel(sink_ref, qa_ref, kva_ref, bias_ref, o_ref, *, seq_len):
    n = pl.program_id(1)
    start = pl.multiple_of(jnp.clip((n - 1) * WIN, 0, seq_len - WIN_SPAN), WIN)
    var = n - start // WIN
    second = lax.broadcasted_iota(jnp.int32, (2 * WIN, 1), 0) >= WIN
    n_pairs = A_HEADS // 2
    kv_lanes = [slice((j // (A_GROUP // 2)) * LANES, (j // (A_GROUP // 2) + 1) * LANES) for j in range(n_pairs)]
    scores = [_dot_nt(_pair_rows(qa_ref[:, j * LANES:(j + 1) * LANES]), kva_ref[pl.ds(start, WIN_SPAN), kv_lanes[j]])
              + bias_ref[var, j] for j in range(n_pairs)]
    sinks = [jnp.where(second, sink_ref[2 * j + 1], sink_ref[2 * j]) for j in range(n_pairs)]
    maxes = [jnp.maximum(jnp.max(s, axis=-1, keepdims=True), sink) for s, sink in zip(scores, sinks)]
    es = [jnp.exp2(s - m) for s, m in zip(scores, maxes)]
    o2s = [_dot(e.astype(BF16), kva_ref[pl.ds(start, WIN_SPAN), 2 * A_KV + kv_lanes[j].start:2 * A_KV + kv_lanes[j].stop])
           for j, e in enumerate(es)]
    for j in range(n_pairs):
        denom = jnp.sum(es[j], axis=-1, keepdims=True) + jnp.exp2(sinks[j] - maxes[j])
        o_ref[:, j * LANES:(j + 1) * LANES] = _unpair_rows(o2s[j] / denom).astype(BF16)


def _win_bias_table():
    off = jnp.arange(3)[:, None, None] * WIN
    dist = jnp.abs(jnp.arange(WIN)[None, :, None] + off - jnp.arange(WIN_SPAN)[None, None, :])
    slopes = jnp.exp2(-8.0 * jnp.arange(1, A_HEADS + 1, dtype=F32) / A_HEADS) * LOG2E
    bias = -slopes[None, :, None, None] * dist[:, None].astype(F32)
    bias = jnp.where((dist <= WIN)[:, None], bias, NEG_INF)
    return bias.reshape(3, A_HEADS // 2, 2 * WIN, WIN_SPAN)


def _win_gqa(qa, kva, sink, bias_tab):
    b, seq_len, _ = qa.shape
    return pl.pallas_call(
        functools.partial(_win_gqa_kernel, seq_len=seq_len),
        out_shape=jax.ShapeDtypeStruct((b, seq_len, A_Q), BF16),
        grid=(b, seq_len // WIN),
        in_specs=[
            pl.BlockSpec(memory_space=pltpu.SMEM),
            pl.BlockSpec((None, WIN, A_Q), lambda i, n: (i, n, 0)),
            pl.BlockSpec((None, seq_len, ATTN_CHUNK), lambda i, n: (i, 0, 0)),
            _const_spec((3, A_HEADS // 2, 2 * WIN, WIN_SPAN)),
        ],
        out_specs=pl.BlockSpec((None, WIN, A_Q), lambda i, n: (i, n, 0)),
        compiler_params=_params("parallel", "arbitrary"),
        name="win_gqa",
    )(sink, qa, kva, bias_tab)


def _nbr_attn_kernel(qb_ref, kb_ref, vb_ref, bias_ref, o_ref, *, n_rows):
    m = pl.program_id(1)
    keys = NA_ROWS * GRID_W

    pairs = [slice(pr * LANES, (pr + 1) * LANES) for pr in range(B_HEADS // 2)]

    def key_offset(i):
        r = NA_QROWS * m + i
        rs = jnp.clip(r - NA_ROWS // 2, 0, n_rows - NA_ROWS)
        return pl.multiple_of(rs * GRID_W, GRID_W), r - rs

    def scores(i):
        off, var = key_offset(i)
        return [_dot_nt(_pair_rows(qb_ref[i * GRID_W:(i + 1) * GRID_W, lanes]), kb_ref[pl.ds(off, keys), lanes])
                + bias_ref[pr, var] for pr, lanes in enumerate(pairs)]

    def finish(i, o2s, denoms):
        for lanes, o2, denom in zip(pairs, o2s, denoms):
            o_ref[i * GRID_W:(i + 1) * GRID_W, lanes] = _unpair_rows(o2 / denom).astype(BF16)

    s_cur, pending = scores(0), None
    for i in range(NA_QROWS):
        s_next = scores(i + 1) if i + 1 < NA_QROWS else None
        off, _ = key_offset(i)
        es = [jnp.exp2(s - jnp.max(s, axis=-1, keepdims=True)) for s in s_cur]
        o2s = [_dot(e.astype(BF16), vb_ref[pl.ds(off, keys), lanes]) for e, lanes in zip(es, pairs)]
        denoms = [jnp.sum(e, axis=-1, keepdims=True) for e in es]
        if pending is not None:
            finish(*pending)
        s_cur, pending = s_next, (i, o2s, denoms)
    finish(*pending)


def _nbr_bias_table(rpb):
    keys = NA_ROWS * GRID_W
    z = jnp.pad(rpb.astype(F32), ((0, 0), (0, 0), (0, GRID_W - (2 * NA_COLS - 1)))).reshape(B_HEADS, -1)
    z = jnp.stack([z[:, (NA_ROWS - 1 - v) * GRID_W:(NA_ROWS - 1 - v) * GRID_W + keys] for v in range(NA_ROWS)], axis=1)
    period = keys + GRID_W
    zz = jnp.concatenate([z[..., NA_COLS - 1:], jnp.zeros((B_HEADS, NA_ROWS, period - keys), F32), z[..., :NA_COLS - 1]], axis=-1)
    skew = jnp.tile(zz, (1, 1, GRID_W))[..., :GRID_W * (period - 1)].reshape(B_HEADS, NA_ROWS, GRID_W, period - 1)
    tab = skew[..., :keys]
    qc = jnp.arange(GRID_W)[:, None]
    kc = jnp.arange(keys)[None, :] % GRID_W
    cs = jnp.clip(qc - NA_COLS // 2, 0, GRID_W - NA_COLS)
    ok = (kc >= cs) & (kc < cs + NA_COLS)
    tab = jnp.where(ok[None, None], tab * LOG2E, NEG_INF)
    tab = tab.reshape(B_HEADS // 2, 2, NA_ROWS, GRID_W, keys)
    return jnp.transpose(tab, (0, 2, 1, 3, 4)).reshape(B_HEADS // 2, NA_ROWS, 2 * GRID_W, keys)


def _nbr_attn(qb, kb, vb, bias_tab):
    b, seq_len, _ = qb.shape
    n_rows = seq_len // GRID_W
    assert n_rows >= NA_ROWS and n_rows % NA_QROWS == 0
    qtile = NA_QROWS * GRID_W
    whole = pl.BlockSpec((None, seq_len, B_QKV), lambda s, m: (s, 0, 0))
    return pl.pallas_call(
        functools.partial(_nbr_attn_kernel, n_rows=n_rows),
        out_shape=jax.ShapeDtypeStruct((b, seq_len, B_QKV), BF16),
        grid=(b, n_rows // NA_QROWS),
        in_specs=[
            pl.BlockSpec((None, qtile, B_QKV), lambda s, m: (s, m, 0)),
            whole, whole,
            _const_spec((B_HEADS // 2, NA_ROWS, 2 * GRID_W, NA_ROWS * GRID_W)),
        ],
        out_specs=pl.BlockSpec((None, qtile, B_QKV), lambda s, m: (s, m, 0)),
        compiler_params=_params("parallel", "arbitrary"),
        name="nbr_attn",
    )(qb, kb, vb, bias_tab)


CONV_SHIFT = CONV_HALO - CONV_WIDTH // 2
CONV_SUBTILES = TOKEN_TILE // CONV_ROWS
CONV_TAP_ROWS = -(-CONV_WIDTH // SUBLANES) * SUBLANES


def _conv_rows(r0, pad_ref, cw_ref, cb_ref, lng_ref, lnb_ref, yd_ref, stage_ref):
    span = CONV_ROWS + SUBLANES
    for lt in range(D_WIDTH // LANES):
        lanes = slice(lt * LANES, (lt + 1) * LANES)
        acc = None
        for b in range(SUBLANES):
            part = None
            for a in range(-(-(CONV_WIDTH + CONV_SHIFT) // SUBLANES)):
                k = SUBLANES * a + b - CONV_SHIFT
                if 0 <= k < CONV_WIDTH:
                    term = cw_ref[k:k + 1, lanes] * pad_ref[r0 + SUBLANES * a:r0 + SUBLANES * a + span, lanes]
                    part = term if part is None else part + term
            part = part[b:b + CONV_ROWS, :]
            acc = part if acc is None else acc + part
        stage_ref[:, lanes] = acc + cb_ref[:, lanes]
    conv = stage_ref[...]
    xc = conv - jnp.mean(conv, axis=-1, keepdims=True)
    y = xc * lax.rsqrt(jnp.mean(xc * xc, axis=-1, keepdims=True) + EPS) * lng_ref[...] + lnb_ref[...]
    yd_ref[r0:r0 + CONV_ROWS, :] = (y * jax.nn.sigmoid(y)).astype(BF16)
    return y


def _pre_ssm_kernel(x_ref, g1_ref, win_ref, wout_ref, g_ref, w_ref, cw_ref, cb_ref, lng_ref, lnb_ref,
                    x1_ref, uc_ref, yd_ref, acc_ref, pad_ref, stage_ref, cws_ref):
    i = pl.program_id(1)
    n_tiles = pl.num_programs(1) - 1
    conv_args = (pad_ref, cw_ref, cb_ref, lng_ref, lnb_ref, yd_ref, stage_ref)
    lo, hi = CONV_HALO, CONV_HALO + TOKEN_TILE

    @pl.when(i == 0)
    def _():
        pad_ref[...] = jnp.zeros_like(pad_ref)

    def early_conv(r, start_zero):
        for rg in range(cws_ref.shape[0] // SUBLANES):
            for lt in range(D_WIDTH // LANES):
                tile = (slice(rg * SUBLANES, (rg + 1) * SUBLANES), slice(lt * LANES, (lt + 1) * LANES))
                cws_ref[tile] = cw_ref[tile] + start_zero
        return _conv_rows(r * CONV_ROWS, pad_ref, cws_ref, *conv_args[2:])

    @pl.when(i < n_tiles)
    def _():
        early = [functools.partial(early_conv, r) for r in range(CONV_SUBTILES - 1)]
        x1 = _ffn_body(x_ref[...], g1_ref, win_ref, wout_ref, acc_ref, side_work=early)
        x1_ref[...] = x1
        xn = _rms(x1, g_ref[...]).astype(BF16)
        uc_ref[...] = _dot(xn, w_ref[:, :C_WIDTH])
        adgd = _dot(xn, w_ref[:, C_WIDTH:])
        gated = adgd[:, :D_WIDTH] * jax.nn.sigmoid(adgd[:, D_WIDTH:])
        pad_ref[hi:, :] = gated[:CONV_HALO, :]
        _conv_rows((CONV_SUBTILES - 1) * CONV_ROWS, *conv_args)
        pad_ref[:lo, :] = pad_ref[hi - CONV_HALO:hi, :]
        pad_ref[lo:hi, :] = gated

    @pl.when(i == n_tiles)
    def _():
        pad_ref[hi:, :] = jnp.zeros((CONV_HALO, D_WIDTH), F32)
        for r in range(CONV_SUBTILES):
            _conv_rows(r * CONV_ROWS, *conv_args)


def _pre_ssm(x, ffn, g, w, conv_w, conv_b, ln_g, ln_b):
    b, seq_len, _ = x.shape
    n_tiles = seq_len // TOKEN_TILE

    def cur(width):
        return pl.BlockSpec((None, TOKEN_TILE, width), lambda s, i: (s, jnp.minimum(i, n_tiles - 1), 0))

    return pl.pallas_call(
        _pre_ssm_kernel,
        out_shape=[jax.ShapeDtypeStruct((b, seq_len, D_MODEL), F32), jax.ShapeDtypeStruct((b, seq_len, C_WIDTH), F32),
                   jax.ShapeDtypeStruct((b, seq_len, D_WIDTH), BF16)],
        grid=(b, n_tiles + 1),
        in_specs=[cur(D_MODEL)] + _ffn_specs() + [
            _const_spec((1, D_MODEL)),
            _const_spec((D_MODEL, C_WIDTH + 2 * D_WIDTH)),
            _const_spec((CONV_TAP_ROWS, D_WIDTH)),
            _const_spec((1, D_WIDTH)),
            _const_spec((1, D_WIDTH)),
            _const_spec((1, D_WIDTH)),
        ],
        out_specs=[cur(D_MODEL), cur(C_WIDTH),
                   pl.BlockSpec((None, TOKEN_TILE, D_WIDTH), lambda s, i: (s, jnp.maximum(i - 1, 0), 0))],
        scratch_shapes=[pltpu.VMEM((TOKEN_TILE, D_MODEL), F32),
                        pltpu.VMEM((TOKEN_TILE + 2 * CONV_HALO, D_WIDTH), F32),
                        pltpu.VMEM((CONV_ROWS, D_WIDTH), F32),
                        pltpu.VMEM((CONV_TAP_ROWS, D_WIDTH), F32)],
        compiler_params=_params("parallel", "arbitrary"),
        name="pre_ssm",
    )(x, *ffn, g, w, conv_w, conv_b, ln_g, ln_b)


SLAB_GROUPS = LANES // SSM_GROUP_CH
N_SLABS = SSM_GROUPS // SLAB_GROUPS
SLAB_HALF = SLAB_GROUPS * SSM_STATE
SLAB = 2 * SLAB_HALF
STATE_W = N_SLABS * SLAB


def _ssm_scan_kernel(uf_ref, ub_ref, b_ref, a_ref, c_ref, yf_ref, yb_ref, s_ref, h_ref):
    c = pl.program_id(1)
    rows = SCAN_CHUNK * SCAN_SEQS

    @pl.when(c == 0)
    def _():
        h_ref[...] = jnp.zeros_like(h_ref)

    for d, u_ref in enumerate((uf_ref, ub_ref)):
        u2d = u_ref[...].reshape(rows, C_WIDTH).astype(BF16)
        for j in range(N_SLABS):
            bu = _dot(u2d[:, j * LANES:(j + 1) * LANES], b_ref[d, j])
            s_ref[d, :, :, j * SLAB:(j + 1) * SLAB] = bu.reshape(SCAN_CHUNK, SCAN_SEQS, SLAB)

    for pair in range(N_SLABS // 2):
        slabs = (2 * pair, 2 * pair + 1)

        def step(i, carry):
            new = []
            for d in range(2):
                t = i if d == 0 else SCAN_CHUNK - 1 - i
                for idx, j in enumerate(slabs):
                    hr, hi = carry[2 * (2 * d + idx)], carry[2 * (2 * d + idx) + 1]
                    re = slice(j * SLAB, j * SLAB + SLAB_HALF)
                    im = slice(j * SLAB + SLAB_HALF, (j + 1) * SLAB)
                    ar = a_ref[d, :, re]
                    ai = a_ref[d, :, im]
                    nr = ar * hr - ai * hi + s_ref[d, t, :, re]
                    ni = ar * hi + ai * hr + s_ref[d, t, :, im]
                    s_ref[d, t, :, re] = nr
                    s_ref[d, t, :, im] = ni
                    new += [nr, ni]
            return tuple(new)

        init = []
        for d in range(2):
            for j in slabs:
                init += [h_ref[d, :, j * SLAB:j * SLAB + SLAB_HALF],
                         h_ref[d, :, j * SLAB + SLAB_HALF:(j + 1) * SLAB]]
        final = lax.fori_loop(0, SCAN_CHUNK, step, tuple(init), unroll=True)
        k = 0
        for d in range(2):
            for j in slabs:
                h_ref[d, :, j * SLAB:j * SLAB + SLAB_HALF] = final[k]
                h_ref[d, :, j * SLAB + SLAB_HALF:(j + 1) * SLAB] = final[k + 1]
                k += 2

    for d, y_ref in enumerate((yf_ref, yb_ref)):
        for j in range(N_SLABS):
            hs = s_ref[d, :, :, j * SLAB:(j + 1) * SLAB].reshape(rows, SLAB).astype(BF16)
            y = _dot(hs, c_ref[d, j])
            y_ref[:, :, j * LANES:(j + 1) * LANES] = y.reshape(SCAN_CHUNK, SCAN_SEQS, LANES)


def _ssm_scan_weights(lam_re, lam_im, log_dt, b_re, b_im, c_re, c_im):
    lam = lax.complex(lam_re.astype(F32), lam_im.astype(F32))
    dt = jnp.exp(log_dt.astype(F32))[:, :, None]
    lam_bar = jnp.exp(lam * dt)
    b_bar = ((lam_bar - 1.0) / lam)[..., None] * lax.complex(b_re.astype(F32), b_im.astype(F32))
    eye = jnp.eye(SLAB_GROUPS, dtype=F32)

    def pack_b(part):
        x = part.reshape(2, N_SLABS, SLAB_GROUPS, SSM_STATE, SSM_GROUP_CH)
        x = jnp.einsum('dsgnp,gh->dsgphn', x, eye)
        return x.reshape(2, N_SLABS, LANES, SLAB_HALF)

    b_pack = jnp.concatenate([pack_b(jnp.real(b_bar)), pack_b(jnp.imag(b_bar))], axis=-1).astype(BF16)

    def pack_a(part):
        return part.reshape(2, N_SLABS, SLAB_HALF)

    a_pack = jnp.concatenate([pack_a(jnp.real(lam_bar)), pack_a(jnp.imag(lam_bar))], axis=-1)
    a_pack = jnp.broadcast_to(a_pack.reshape(2, 1, STATE_W), (2, SCAN_SEQS, STATE_W))

    def pack_c(part):
        x = part.astype(F32).reshape(2, N_SLABS, SLAB_GROUPS, SSM_GROUP_CH, SSM_STATE)
        x = jnp.einsum('dsgpn,gh->dsgnhp', x, eye)
        return x.reshape(2, N_SLABS, SLAB_HALF, LANES)

    c_pack = jnp.concatenate([pack_c(c_re), -pack_c(c_im)], axis=2).astype(BF16)
    return b_pack, a_pack, c_pack


def _ssm_scan(u_tm, b_pack, a_pack, c_pack):
    seq_len, b, _ = u_tm.shape
    nc = seq_len // SCAN_CHUNK
    blk = (SCAN_CHUNK, SCAN_SEQS, C_WIDTH)
    return pl.pallas_call(
        _ssm_scan_kernel,
        out_shape=[jax.ShapeDtypeStruct(u_tm.shape, F32)] * 2,
        grid=(b // SCAN_SEQS, nc),
        in_specs=[
            pl.BlockSpec(blk, lambda s, c: (c, s, 0)),
            pl.BlockSpec(blk, lambda s, c: (nc - 1 - c, s, 0)),
            _const_spec((2, N_SLABS, LANES, SLAB)),
            _const_spec((2, SCAN_SEQS, STATE_W)),
            _const_spec((2, N_SLABS, SLAB, LANES)),
        ],
        out_specs=[
            pl.BlockSpec(blk, lambda s, c: (c, s, 0)),
            pl.BlockSpec(blk, lambda s, c: (nc - 1 - c, s, 0)),
        ],
        scratch_shapes=[pltpu.VMEM((2, SCAN_CHUNK, SCAN_SEQS, STATE_W), F32),
                        pltpu.VMEM((2, SCAN_SEQS, STATE_W), F32)],
        compiler_params=_params("parallel", "arbitrary"),
        name="ssm_scan",
    )(u_tm, u_tm, b_pack, a_pack, c_pack)


def _row(v):
    return v.astype(F32).reshape(1, -1)


def _attn_layer(x, ffn1, g_mix, w, j):
    w_in, gain = w['attn_in'][j]
    x1, qa, kva, qb, kb, vb = _pre_attn(x, ffn1, g_mix, w_in, gain)
    ya = _win_gqa(qa, kva, w['sink_a'][j].astype(F32) * LOG2E, w['win_bias'])
    yb = _nbr_attn(qb, kb, vb, w['nbr_bias'][j])
    return x1, (ya, yb, w['w_attn_out'][j])


def _ssm_layer(x, ffn1, g_mix, w, j):
    conv_w = jnp.pad(w['conv_w'][j].astype(F32), ((0, CONV_TAP_ROWS - CONV_WIDTH), (0, 0)))
    x1, uc, yd = _pre_ssm(x, ffn1, g_mix, w['w_ssm_in'][j], conv_w, _row(w['conv_b'][j]),
                          _row(w['ln_g_d'][j]), _row(w['ln_b_d'][j]))
    b_pack, a_pack, c_pack = w['scan'][j]
    yf, yb = _ssm_scan(jnp.transpose(uc, (1, 0, 2)), b_pack, a_pack, c_pack)
    yf = jnp.transpose(yf, (1, 0, 2))
    yb = jnp.transpose(yb, (1, 0, 2))
    return x1, (yf, yb, uc, yd, _row(w['d_skip'][j]), w['w_glu_c'][j], _row(w['b_glu_c'][j]), w['w_ssm_out'][j])


def _trunk(x, p, w):
    b, seq_len, _ = x.shape
    assert seq_len % TOKEN_TILE == 0 and seq_len % SCAN_CHUNK == 0 and b % SCAN_SEQS == 0
    assert seq_len >= WIN_SPAN and seq_len % WIN == 0
    for i in range(DEPTH):
        ffn1 = (_row(w['norm_ffn1'][i]), w['w_ffn1_in'][i], w['w_ffn1_out'][i])
        ffn2 = (_row(w['norm_ffn2'][i]), w['w_ffn2_in'][i], w['w_ffn2_out'][i])
        ple = (_row(w['norm_ple'][i]), w['w_ple_gate'][i], w['w_ple_proj'][i], _row(w['norm_ple_post'][i]))
        layer = _attn_layer if i % 2 == 0 else _ssm_layer
        x, mix = layer(x, ffn1, _row(w['norm_mix'][i]), w, i // 2)
        x = _post(x, mix, ffn2, p, i, ple)
    return x


def kernel(x_prompt, x_sample, p_prompt, p_sample, norm_ffn1, w_ffn1_in, w_ffn1_out, norm_mix, norm_ffn2, w_ffn2_in, w_ffn2_out, norm_ple, w_ple_gate, w_ple_proj, norm_ple_post, w_attn_in, q_gain_a, k_gain_a, sink_a, q_gain_b, k_gain_b, rpb_b, w_attn_out, w_ssm_in, lam_re, lam_im, log_dt, b_re, b_im, c_re, c_im, d_skip, w_glu_c, b_glu_c, conv_w, conv_b, ln_g_d, ln_b_d, w_ssm_out):
    w = dict(norm_ffn1=norm_ffn1, norm_mix=norm_mix, norm_ffn2=norm_ffn2, norm_ple=norm_ple,
             norm_ple_post=norm_ple_post, q_gain_a=q_gain_a, k_gain_a=k_gain_a, sink_a=sink_a,
             q_gain_b=q_gain_b, k_gain_b=k_gain_b, d_skip=d_skip, b_glu_c=b_glu_c, conv_w=conv_w,
             conv_b=conv_b, ln_g_d=ln_g_d, ln_b_d=ln_b_d)
    for name, val in dict(w_ffn1_in=w_ffn1_in, w_ffn1_out=w_ffn1_out, w_ffn2_in=w_ffn2_in,
                          w_ffn2_out=w_ffn2_out, w_ple_gate=w_ple_gate, w_ple_proj=w_ple_proj,
                          w_attn_out=w_attn_out, w_ssm_in=w_ssm_in,
                          w_glu_c=w_glu_c, w_ssm_out=w_ssm_out).items():
        w[name] = val.astype(BF16)
    w['nbr_bias'] = [_nbr_bias_table(rpb_b[j]) for j in range(rpb_b.shape[0])]
    w['win_bias'] = _win_bias_table()
    w['attn_in'] = [_attn_in_weights(w_attn_in[j], q_gain_a[j], k_gain_a[j], q_gain_b[j], k_gain_b[j])
                    for j in range(w_attn_in.shape[0])]
    w['scan'] = [_ssm_scan_weights(lam_re[j], lam_im[j], log_dt[j], b_re[j], b_im[j], c_re[j], c_im[j])
                 for j in range(lam_re.shape[0])]
    return (_trunk(x_prompt, p_prompt, w), _trunk(x_sample, p_sample, w))
```

```python
import functools
import math

import jax
import jax.numpy as jnp
from jax import lax
from jax.experimental import pallas as pl
from jax.experimental.pallas import tpu as pltpu

D_MODEL = 1024
DEPTH = 4
HEAD_DIM = 64
A_HEADS = 8
A_KV_HEADS = 2
A_GROUP = A_HEADS // A_KV_HEADS
WIN = 128
B_HEADS = 8
GRID_W = 64
NA_ROWS = 8
NA_COLS = 16
C_WIDTH = 512
SSM_GROUP_CH = 16
SSM_GROUPS = C_WIDTH // SSM_GROUP_CH
SSM_STATE = 64
D_WIDTH = 512
CONV_WIDTH = 31
D_FF = 2816
PLE_DIM = 256
A_Q = A_HEADS * HEAD_DIM
A_KV = A_KV_HEADS * HEAD_DIM
B_QKV = B_HEADS * HEAD_DIM
ATTN_IN = A_Q + 2 * A_KV + 3 * B_QKV
NEG_INF = -1e30
EPS = 1e-6
LOG2E = math.log2(math.e)

BF16 = jnp.bfloat16
F32 = jnp.float32

VMEM_LIMIT_BYTES = 52 * 1024 * 1024
LANES = 128
SUBLANES = 8
TOKEN_TILE = 512
FF_CHUNK = 256
SCAN_CHUNK = 64
SCAN_SEQS = SUBLANES
NA_QROWS = 8
CONV_HALO = 16
CONV_ROWS = 64


def _params(*sem):
    return pltpu.CompilerParams(dimension_semantics=sem, vmem_limit_bytes=VMEM_LIMIT_BYTES)


def _rms(x, g):
    return x * lax.rsqrt(jnp.mean(x * x, axis=-1, keepdims=True) + EPS) * g


def _dot(a, b):
    return jnp.dot(a, b, preferred_element_type=F32)


def _dot_nt(a, b):
    return lax.dot_general(a, b, (((1,), (1,)), ((), ())), preferred_element_type=F32)


def _const_spec(shape):
    nd = len(shape)
    return pl.BlockSpec(shape, lambda *_: (0,) * nd, pipeline_mode=pl.Buffered(1))


def _ffn_body(x, g_ref, win_ref, wout_ref, acc_ref):
    xn = _rms(x, g_ref[...]).astype(BF16)
    for c in range(D_FF // FF_CHUNK):
        lo = c * FF_CHUNK
        gate = _dot(xn, win_ref[:, lo:lo + FF_CHUNK])
        up = _dot(xn, win_ref[:, D_FF + lo:D_FF + lo + FF_CHUNK])
        act = (gate * jax.nn.sigmoid(gate) * up).astype(BF16)
        part = _dot(act, wout_ref[lo:lo + FF_CHUNK, :])
        if c == 0:
            acc_ref[...] = part
        else:
            acc_ref[...] += part
    return x + 0.5 * acc_ref[...]


def _ple_body(x, p_ref, g1_ref, wg_ref, wp_ref, g2_ref):
    gate = jax.nn.sigmoid(_dot(_rms(x, g1_ref[...]).astype(BF16), wg_ref[...]))
    proj = _dot(p_ref[...].astype(BF16), wp_ref[...])
    return x + gate * _rms(proj, g2_ref[...])


def _tok_spec(width):
    return pl.BlockSpec((None, TOKEN_TILE, width), lambda s, i: (s, i, 0))


def _ffn_specs():
    return [_const_spec((1, D_MODEL)), _const_spec((D_MODEL, 2 * D_FF)), _const_spec((D_FF, D_MODEL))]


def _ple_specs():
    return [_const_spec((1, D_MODEL)), _const_spec((D_MODEL, D_MODEL)), _const_spec((PLE_DIM, D_MODEL)),
            _const_spec((1, D_MODEL))]


def _gelu_tanh(x):
    return 0.5 * x * (1.0 + jnp.tanh(math.sqrt(2.0 / math.pi) * (x + 0.044715 * (x * x * x))))


def _post_kernel(*refs, n_mix):
    x_ref = refs[0]
    mix_refs = refs[1:1 + n_mix]
    ffn_refs = refs[1 + n_mix:4 + n_mix]
    p_ref = refs[4 + n_mix]
    ple_refs = refs[5 + n_mix:9 + n_mix]
    o_ref, acc_ref = refs[9 + n_mix:]
    x = x_ref[...]
    if n_mix:
        ya_ref, yb_ref, w_ref = mix_refs
        half = ya_ref.shape[-1]
        x = x + _dot(ya_ref[...], w_ref[:half, :]) + _dot(yb_ref[...], w_ref[half:, :])
    x = _ffn_body(x, *ffn_refs, acc_ref)
    o_ref[...] = _ple_body(x, p_ref, *ple_refs)


def _post(x, mix, ffn, p, layer, ple):
    b, seq_len, _ = x.shape
    mix_specs = [_tok_spec(mix[0].shape[-1]), _tok_spec(mix[1].shape[-1]), _const_spec(mix[2].shape)] if mix else []
    return pl.pallas_call(
        functools.partial(_post_kernel, n_mix=len(mix)),
        out_shape=jax.ShapeDtypeStruct(x.shape, F32),
        grid=(b, seq_len // TOKEN_TILE),
        in_specs=([_tok_spec(D_MODEL)] + mix_specs + _ffn_specs()
                  + [pl.BlockSpec((None, None, TOKEN_TILE, PLE_DIM), lambda s, i: (layer, s, i, 0))] + _ple_specs()),
        out_specs=_tok_spec(D_MODEL),
        scratch_shapes=[pltpu.VMEM((TOKEN_TILE, D_MODEL), F32)],
        compiler_params=_params("parallel", "parallel"),
        name="post",
    )(x, *mix, *ffn, p, *ple)


ATTN_CHUNK = 512
ATTN_COLS = 5 * ATTN_CHUNK
ATTN_NORM_PIECES = (0, 1, 2, 4, 5, 6, 7)
NORM_PIECE = 256


def _pre_attn_kernel(x_ref, g1_ref, win_ref, wout_ref, g_ref, w_ref, gain_ref, ones_ref,
                     x1_ref, qa_ref, kva_ref, qb_ref, kb_ref, vb_ref, acc_ref):
    x1 = _ffn_body(x_ref[...], g1_ref, win_ref, wout_ref, acc_ref)
    x1_ref[...] = x1
    xn = _rms(x1, g_ref[...]).astype(BF16)
    for c, ref in enumerate((qa_ref, kva_ref, qb_ref, kb_ref, vb_ref)):
        h = _dot(xn, w_ref[:, c * ATTN_CHUNK:(c + 1) * ATTN_CHUNK])
        for half in range(ATTN_CHUNK // NORM_PIECE):
            piece = c * (ATTN_CHUNK // NORM_PIECE) + half
            lanes = slice(half * NORM_PIECE, (half + 1) * NORM_PIECE)
            hp = h[:, lanes]
            if piece in ATTN_NORM_PIECES:
                ss = _dot((hp * hp).astype(BF16), ones_ref[...])
                hp = hp * lax.rsqrt(ss * (1.0 / HEAD_DIM) + EPS) * gain_ref[:, piece * NORM_PIECE:(piece + 1) * NORM_PIECE]
            ref[:, lanes] = hp.astype(BF16)


def _attn_in_weights(w, q_gain_a, k_gain_a, q_gain_b, k_gain_b):
    ka = [w[:, A_Q + h * HEAD_DIM:A_Q + (h + 1) * HEAD_DIM] for h in range(A_KV_HEADS)]
    va = [w[:, A_Q + A_KV + h * HEAD_DIM:A_Q + A_KV + (h + 1) * HEAD_DIM] for h in range(A_KV_HEADS)]
    w2 = jnp.concatenate([w[:, :A_Q]] + [c for h in range(A_KV_HEADS) for c in (ka[h], ka[h])]
                         + [c for h in range(A_KV_HEADS) for c in (va[h], va[h])]
                         + [w[:, A_Q + 2 * A_KV:]], axis=1).astype(BF16)
    scale = HEAD_DIM ** -0.5 * LOG2E
    f = lambda g, reps, s=1.0: jnp.tile(g.astype(F32) * s, reps)
    gain = jnp.concatenate([f(q_gain_a, A_HEADS, scale), f(k_gain_a, 2 * A_KV_HEADS), jnp.ones((2 * A_KV,), F32),
                            f(q_gain_b, B_HEADS, scale), f(k_gain_b, B_HEADS), jnp.ones((B_QKV,), F32)])
    return w2, gain.reshape(1, ATTN_COLS)


def _head_ones():
    idx = jnp.arange(NORM_PIECE) // HEAD_DIM
    return (idx[:, None] == idx[None, :]).astype(BF16)


def _pre_attn(x, ffn, g, w2, gain):
    b, seq_len, _ = x.shape
    return pl.pallas_call(
        _pre_attn_kernel,
        out_shape=[jax.ShapeDtypeStruct(x.shape, F32)] + [jax.ShapeDtypeStruct((b, seq_len, ATTN_CHUNK), BF16)] * 5,
        grid=(b, seq_len // TOKEN_TILE),
        in_specs=[_tok_spec(D_MODEL)] + _ffn_specs() + [
            _const_spec((1, D_MODEL)),
            _const_spec((D_MODEL, ATTN_COLS)),
            _const_spec((1, ATTN_COLS)),
            _const_spec((NORM_PIECE, NORM_PIECE)),
        ],
        out_specs=[_tok_spec(D_MODEL)] + [_tok_spec(ATTN_CHUNK)] * 5,
        scratch_shapes=[pltpu.VMEM((TOKEN_TILE, D_MODEL), F32)],
        compiler_params=_params("parallel", "parallel"),
        name="pre_attn",
    )(x, *ffn, g, w2, gain, _head_ones())


def _pair_rows(q_tile):
    upper = lax.broadcasted_iota(jnp.int32, q_tile.shape, 1) >= HEAD_DIM
    zero = jnp.zeros_like(q_tile)
    return jnp.concatenate([jnp.where(upper, zero, q_tile), jnp.where(upper, q_tile, zero)], axis=0)


def _unpair_rows(o2):
    r = o2.shape[0] // 2
    upper = lax.broadcasted_iota(jnp.int32, (r, LANES), 1) >= HEAD_DIM
    return jnp.where(upper, o2[r:], o2[:r])


WIN_SPAN = 3 * WIN


def _win_gqa_kernel(sink_ref, qa_ref, kva_ref, bias_ref, o_ref, *, seq_len):
    n = pl.program_id(1)
    start = pl.multiple_of(jnp.clip((n - 1) * WIN, 0, seq_len - WIN_SPAN), WIN)
    var = n - start // WIN
    second = lax.broadcasted_iota(jnp.int32, (2 * WIN, 1), 0) >= WIN
    n_pairs = A_HEADS // 2
    kv_lanes = [slice((j // (A_GROUP // 2)) * LANES, (j // (A_GROUP // 2) + 1) * LANES) for j in range(n_pairs)]
    scores = [_dot_nt(_pair_rows(qa_ref[:, j * LANES:(j + 1) * LANES]), kva_ref[pl.ds(start, WIN_SPAN), kv_lanes[j]])
              + bias_ref[var, j] for j in range(n_pairs)]
    sinks = [jnp.where(second, sink_ref[2 * j + 1], sink_ref[2 * j]) for j in range(n_pairs)]
    maxes = [jnp.maximum(jnp.max(s, axis=-1, keepdims=True), sink) for s, sink in zip(scores, sinks)]
    es = [jnp.exp2(s - m) for s, m in zip(scores, maxes)]
    o2s = [_dot(e.astype(BF16), kva_ref[pl.ds(start, WIN_SPAN), 2 * A_KV + kv_lanes[j].start:2 * A_KV + kv_lanes[j].stop])
           for j, e in enumerate(es)]
    for j in range(n_pairs):
        denom = jnp.sum(es[j], axis=-1, keepdims=True) + jnp.exp2(sinks[j] - maxes[j])
        o_ref[:, j * LANES:(j + 1) * LANES] = _unpair_rows(o2s[j] / denom).astype(BF16)


def _win_bias_table():
    off = jnp.arange(3)[:, None, None] * WIN
    dist = jnp.abs(jnp.arange(WIN)[None, :, None] + off - jnp.arange(WIN_SPAN)[None, None, :])
    slopes = jnp.exp2(-8.0 * jnp.arange(1, A_HEADS + 1, dtype=F32) / A_HEADS) * LOG2E
    bias = -slopes[None, :, None, None] * dist[:, None].astype(F32)
    bias = jnp.where((dist <= WIN)[:, None], bias, NEG_INF)
    return bias.reshape(3, A_HEADS // 2, 2 * WIN, WIN_SPAN)


def _win_gqa(qa, kva, sink, bias_tab):
    b, seq_len, _ = qa.shape
    return pl.pallas_call(
        functools.partial(_win_gqa_kernel, seq_len=seq_len),
        out_shape=jax.ShapeDtypeStruct((b, seq_len, A_Q), BF16),
        grid=(b, seq_len // WIN),
        in_specs=[
            pl.BlockSpec(memory_space=pltpu.SMEM),
            pl.BlockSpec((None, WIN, A_Q), lambda i, n: (i, n, 0)),
            pl.BlockSpec((None, seq_len, ATTN_CHUNK), lambda i, n: (i, 0, 0)),
            _const_spec((3, A_HEADS // 2, 2 * WIN, WIN_SPAN)),
        ],
        out_specs=pl.BlockSpec((None, WIN, A_Q), lambda i, n: (i, n, 0)),
        compiler_params=_params("parallel", "arbitrary"),
        name="win_gqa",
    )(sink, qa, kva, bias_tab)


def _nbr_attn_kernel(qb_ref, kb_ref, vb_ref, bias_ref, o_ref, *, n_rows):
    m = pl.program_id(1)
    keys = NA_ROWS * GRID_W

    pairs = [slice(pr * LANES, (pr + 1) * LANES) for pr in range(B_HEADS // 2)]

    def key_offset(i):
        r = NA_QROWS * m + i
        rs = jnp.clip(r - NA_ROWS // 2, 0, n_rows - NA_ROWS)
        return pl.multiple_of(rs * GRID_W, GRID_W), r - rs

    def scores(i):
        off, var = key_offset(i)
        return [_dot_nt(_pair_rows(qb_ref[i * GRID_W:(i + 1) * GRID_W, lanes]), kb_ref[pl.ds(off, keys), lanes])
                + bias_ref[pr, var] for pr, lanes in enumerate(pairs)]

    def finish(i, o2s, denoms):
        for lanes, o2, denom in zip(pairs, o2s, denoms):
            o_ref[i * GRID_W:(i + 1) * GRID_W, lanes] = _unpair_rows(o2 / denom).astype(BF16)

    s_cur, pending = scores(0), None
    for i in range(NA_QROWS):
        s_next = scores(i + 1) if i + 1 < NA_QROWS else None
        off, _ = key_offset(i)
        es = [jnp.exp2(s - jnp.max(s, axis=-1, keepdims=True)) for s in s_cur]
        o2s = [_dot(e.astype(BF16), vb_ref[pl.ds(off, keys), lanes]) for e, lanes in zip(es, pairs)]
        denoms = [jnp.sum(e, axis=-1, keepdims=True) for e in es]
        if pending is not None:
            finish(*pending)
        s_cur, pending = s_next, (i, o2s, denoms)
    finish(*pending)


def _nbr_bias_table(rpb):
    keys = NA_ROWS * GRID_W
    z = jnp.pad(rpb.astype(F32), ((0, 0), (0, 0), (0, GRID_W - (2 * NA_COLS - 1)))).reshape(B_HEADS, -1)
    z = jnp.stack([z[:, (NA_ROWS - 1 - v) * GRID_W:(NA_ROWS - 1 - v) * GRID_W + keys] for v in range(NA_ROWS)], axis=1)
    period = keys + GRID_W
    zz = jnp.concatenate([z[..., NA_COLS - 1:], jnp.zeros((B_HEADS, NA_ROWS, period - keys), F32), z[..., :NA_COLS - 1]], axis=-1)
    skew = jnp.tile(zz, (1, 1, GRID_W))[..., :GRID_W * (period - 1)].reshape(B_HEADS, NA_ROWS, GRID_W, period - 1)
    tab = skew[..., :keys]
    qc = jnp.arange(GRID_W)[:, None]
    kc = jnp.arange(keys)[None, :] % GRID_W
    cs = jnp.clip(qc - NA_COLS // 2, 0, GRID_W - NA_COLS)
    ok = (kc >= cs) & (kc < cs + NA_COLS)
    tab = jnp.where(ok[None, None], tab * LOG2E, NEG_INF)
    tab = tab.reshape(B_HEADS // 2, 2, NA_ROWS, GRID_W, keys)
    return jnp.transpose(tab, (0, 2, 1, 3, 4)).reshape(B_HEADS // 2, NA_ROWS, 2 * GRID_W, keys)


def _nbr_attn(qb, kb, vb, bias_tab):
    b, seq_len, _ = qb.shape
    n_rows = seq_len // GRID_W
    assert n_rows >= NA_ROWS and n_rows % NA_QROWS == 0
    qtile = NA_QROWS * GRID_W
    whole = pl.BlockSpec((None, seq_len, B_QKV), lambda s, m: (s, 0, 0))
    return pl.pallas_call(
        functools.partial(_nbr_attn_kernel, n_rows=n_rows),
        out_shape=jax.ShapeDtypeStruct((b, seq_len, B_QKV), BF16),
        grid=(b, n_rows // NA_QROWS),
        in_specs=[
            pl.BlockSpec((None, qtile, B_QKV), lambda s, m: (s, m, 0)),
            whole, whole,
            _const_spec((B_HEADS // 2, NA_ROWS, 2 * GRID_W, NA_ROWS * GRID_W)),
        ],
        out_specs=pl.BlockSpec((None, qtile, B_QKV), lambda s, m: (s, m, 0)),
        compiler_params=_params("parallel", "arbitrary"),
        name="nbr_attn",
    )(qb, kb, vb, bias_tab)


CONV_SHIFT = CONV_HALO - CONV_WIDTH // 2


def _conv_rows(r0, pad_ref, cw_ref, cb_ref, lng_ref, lnb_ref, yd_ref, stage_ref):
    span = CONV_ROWS + SUBLANES
    for lt in range(D_WIDTH // LANES):
        lanes = slice(lt * LANES, (lt + 1) * LANES)
        acc = None
        for b in range(SUBLANES):
            part = None
            for a in range(-(-(CONV_WIDTH + CONV_SHIFT) // SUBLANES)):
                k = SUBLANES * a + b - CONV_SHIFT
                if 0 <= k < CONV_WIDTH:
                    term = cw_ref[k:k + 1, lanes] * pad_ref[r0 + SUBLANES * a:r0 + SUBLANES * a + span, lanes]
                    part = term if part is None else part + term
            part = part[b:b + CONV_ROWS, :]
            acc = part if acc is None else acc + part
        stage_ref[:, lanes] = acc + cb_ref[:, lanes]
    conv = stage_ref[...]
    xc = conv - jnp.mean(conv, axis=-1, keepdims=True)
    y = xc * lax.rsqrt(jnp.mean(xc * xc, axis=-1, keepdims=True) + EPS) * lng_ref[...] + lnb_ref[...]
    yd_ref[r0:r0 + CONV_ROWS, :] = (y * jax.nn.sigmoid(y)).astype(BF16)


def _pre_ssm_kernel(x_ref, g1_ref, win_ref, wout_ref, g_ref, w_ref, x1_ref, uc_ref, adgd_ref, acc_ref):
    x1 = _ffn_body(x_ref[...], g1_ref, win_ref, wout_ref, acc_ref)
    x1_ref[...] = x1
    xn = _rms(x1, g_ref[...]).astype(BF16)
    uc_ref[...] = _dot(xn, w_ref[:, :C_WIDTH])
    adgd_ref[...] = _dot(xn, w_ref[:, C_WIDTH:])


def _pre_ssm(x, ffn, g, w):
    b, seq_len, _ = x.shape
    widths = (D_MODEL, C_WIDTH, 2 * D_WIDTH)
    return pl.pallas_call(
        _pre_ssm_kernel,
        out_shape=[jax.ShapeDtypeStruct((b, seq_len, n), F32) for n in widths],
        grid=(b, seq_len // TOKEN_TILE),
        in_specs=[_tok_spec(D_MODEL)] + _ffn_specs() + [
            _const_spec((1, D_MODEL)),
            _const_spec((D_MODEL, C_WIDTH + 2 * D_WIDTH)),
        ],
        out_specs=[_tok_spec(n) for n in widths],
        scratch_shapes=[pltpu.VMEM((TOKEN_TILE, D_MODEL), F32)],
        compiler_params=_params("parallel", "parallel"),
        name="pre_ssm",
    )(x, *ffn, g, w)


SLAB_GROUPS = LANES // SSM_GROUP_CH
N_SLABS = SSM_GROUPS // SLAB_GROUPS
SLAB_HALF = SLAB_GROUPS * SSM_STATE
SLAB = 2 * SLAB_HALF
STATE_W = N_SLABS * SLAB


def _ssm_scan_kernel(uf_ref, ub_ref, perm_ref, permt_ref, b_ref, a_ref, c_ref, yf_ref, yb_ref, s_ref, h_ref, yt_ref):
    c = pl.program_id(1)
    rows = SCAN_CHUNK * SCAN_SEQS

    @pl.when(c == 0)
    def _():
        h_ref[...] = jnp.zeros_like(h_ref)

    for d, u_ref in enumerate((uf_ref, ub_ref)):
        u2d = _dot(perm_ref[...], u_ref[...].reshape(rows, C_WIDTH).astype(BF16)).astype(BF16)
        for j in range(N_SLABS):
            bu = _dot(u2d[:, j * LANES:(j + 1) * LANES], b_ref[d, j])
            s_ref[d, :, :, j * SLAB:(j + 1) * SLAB] = bu.reshape(SCAN_CHUNK, SCAN_SEQS, SLAB)

    for pair in range(N_SLABS // 2):
        slabs = (2 * pair, 2 * pair + 1)

        def step(i, carry):
            new = []
            for d in range(2):
                t = i if d == 0 else SCAN_CHUNK - 1 - i
                for idx, j in enumerate(slabs):
                    hr, hi = carry[2 * (2 * d + idx)], carry[2 * (2 * d + idx) + 1]
                    re = slice(j * SLAB, j * SLAB + SLAB_HALF)
                    im = slice(j * SLAB + SLAB_HALF, (j + 1) * SLAB)
                    ar = a_ref[d, :, re]
                    ai = a_ref[d, :, im]
                    nr = ar * hr - ai * hi + s_ref[d, t, :, re]
                    ni = ar * hi + ai * hr + s_ref[d, t, :, im]
                    s_ref[d, t, :, re] = nr
                    s_ref[d, t, :, im] = ni
                    new += [nr, ni]
            return tuple(new)

        init = []
        for d in range(2):
            for j in slabs:
                init += [h_ref[d, :, j * SLAB:j * SLAB + SLAB_HALF],
                         h_ref[d, :, j * SLAB + SLAB_HALF:(j + 1) * SLAB]]
        final = lax.fori_loop(0, SCAN_CHUNK, step, tuple(init), unroll=True)
        k = 0
        for d in range(2):
            for j in slabs:
                h_ref[d, :, j * SLAB:j * SLAB + SLAB_HALF] = final[k]
                h_ref[d, :, j * SLAB + SLAB_HALF:(j + 1) * SLAB] = final[k + 1]
                k += 2

    for d, y_ref in enumerate((yf_ref, yb_ref)):
        for j in range(N_SLABS):
            hs = s_ref[d, :, :, j * SLAB:(j + 1) * SLAB].reshape(rows, SLAB).astype(BF16)
            yt_ref[:, j * LANES:(j + 1) * LANES] = _dot(hs, c_ref[d, j])
        y_sb = _dot(permt_ref[...], yt_ref[...].astype(BF16))
        y_ref[...] = y_sb.astype(BF16).reshape(SCAN_SEQS, SCAN_CHUNK, C_WIDTH)


def _ssm_scan_weights(lam_re, lam_im, log_dt, b_re, b_im, c_re, c_im):
    lam = lax.complex(lam_re.astype(F32), lam_im.astype(F32))
    dt = jnp.exp(log_dt.astype(F32))[:, :, None]
    lam_bar = jnp.exp(lam * dt)
    b_bar = ((lam_bar - 1.0) / lam)[..., None] * lax.complex(b_re.astype(F32), b_im.astype(F32))
    eye = jnp.eye(SLAB_GROUPS, dtype=F32)

    def pack_b(part):
        x = part.reshape(2, N_SLABS, SLAB_GROUPS, SSM_STATE, SSM_GROUP_CH)
        x = jnp.einsum('dsgnp,gh->dsgphn', x, eye)
        return x.reshape(2, N_SLABS, LANES, SLAB_HALF)

    b_pack = jnp.concatenate([pack_b(jnp.real(b_bar)), pack_b(jnp.imag(b_bar))], axis=-1).astype(BF16)

    def pack_a(part):
        return part.reshape(2, N_SLABS, SLAB_HALF)

    a_pack = jnp.concatenate([pack_a(jnp.real(lam_bar)), pack_a(jnp.imag(lam_bar))], axis=-1)
    a_pack = jnp.broadcast_to(a_pack.reshape(2, 1, STATE_W), (2, SCAN_SEQS, STATE_W))

    def pack_c(part):
        x = part.astype(F32).reshape(2, N_SLABS, SLAB_GROUPS, SSM_GROUP_CH, SSM_STATE)
        x = jnp.einsum('dsgpn,gh->dsgnhp', x, eye)
        return x.reshape(2, N_SLABS, SLAB_HALF, LANES)

    c_pack = jnp.concatenate([pack_c(c_re), -pack_c(c_im)], axis=2).astype(BF16)
    return b_pack, a_pack, c_pack


def _ssm_scan(u, b_pack, a_pack, c_pack):
    b, seq_len, _ = u.shape
    nc = seq_len // SCAN_CHUNK
    rows = SCAN_CHUNK * SCAN_SEQS
    blk = (SCAN_SEQS, SCAN_CHUNK, C_WIDTH)
    r = jnp.arange(rows)
    perm = (r[:, None] == (r[None, :] % SCAN_CHUNK) * SCAN_SEQS + r[None, :] // SCAN_CHUNK).astype(BF16)
    return pl.pallas_call(
        _ssm_scan_kernel,
        out_shape=[jax.ShapeDtypeStruct(u.shape, BF16)] * 2,
        grid=(b // SCAN_SEQS, nc),
        in_specs=[
            pl.BlockSpec(blk, lambda s, c: (s, c, 0)),
            pl.BlockSpec(blk, lambda s, c: (s, nc - 1 - c, 0)),
            _const_spec((rows, rows)),
            _const_spec((rows, rows)),
            _const_spec((2, N_SLABS, LANES, SLAB)),
            _const_spec((2, SCAN_SEQS, STATE_W)),
            _const_spec((2, N_SLABS, SLAB, LANES)),
        ],
        out_specs=[
            pl.BlockSpec(blk, lambda s, c: (s, c, 0)),
            pl.BlockSpec(blk, lambda s, c: (s, nc - 1 - c, 0)),
        ],
        scratch_shapes=[pltpu.VMEM((2, SCAN_CHUNK, SCAN_SEQS, STATE_W), F32),
                        pltpu.VMEM((2, SCAN_SEQS, STATE_W), F32),
                        pltpu.VMEM((rows, C_WIDTH), F32)],
        compiler_params=_params("parallel", "arbitrary"),
        name="ssm_scan",
    )(u, u, perm, perm.T, b_pack, a_pack, c_pack)


def _ssm_out_kernel(x_ref, yf_ref, yb_ref, uc_ref, cur_ref, prev_ref, next_ref,
                    dskip_ref, wglu_ref, bglu_ref, cw_ref, cb_ref, lng_ref, lnb_ref, w_ref,
                    o_ref, pad_ref, yd_ref, stage_ref):
    i = pl.program_id(1)
    last = pl.num_programs(1) - 1
    tile = cur_ref.shape[0]

    z = _gelu_tanh(yf_ref[...].astype(F32) + yb_ref[...].astype(F32) + dskip_ref[...] * uc_ref[...])
    yc = z * jax.nn.sigmoid(_dot(z.astype(BF16), wglu_ref[...]) + bglu_ref[...])

    def glu(ref):
        v = ref[...]
        return v[:, :D_WIDTH] * jax.nn.sigmoid(v[:, D_WIDTH:])

    pad_ref[0:CONV_HALO, :] = jnp.where(i > 0, glu(prev_ref), 0.0)
    pad_ref[CONV_HALO:CONV_HALO + tile, :] = glu(cur_ref)
    pad_ref[CONV_HALO + tile:, :] = jnp.where(i < last, glu(next_ref), 0.0)
    for r0 in range(0, tile, CONV_ROWS):
        _conv_rows(r0, pad_ref, cw_ref, cb_ref, lng_ref, lnb_ref, yd_ref, stage_ref)

    o_ref[...] = (x_ref[...]
                  + _dot(yc.astype(BF16), w_ref[:C_WIDTH, :])
                  + _dot(yd_ref[...], w_ref[C_WIDTH:, :]))


def _ssm_out(x, yf, yb, uc, adgd, d_skip, w_glu, b_glu, conv_w, conv_b, ln_g, ln_b, w_out):
    b, seq_len, _ = x.shape
    per = TOKEN_TILE // CONV_HALO
    n_halo = seq_len // CONV_HALO
    return pl.pallas_call(
        _ssm_out_kernel,
        out_shape=jax.ShapeDtypeStruct(x.shape, F32),
        grid=(b, seq_len // TOKEN_TILE),
        in_specs=[
            _tok_spec(D_MODEL), _tok_spec(C_WIDTH), _tok_spec(C_WIDTH), _tok_spec(C_WIDTH), _tok_spec(2 * D_WIDTH),
            pl.BlockSpec((None, CONV_HALO, 2 * D_WIDTH), lambda s, i: (s, jnp.maximum(i * per - 1, 0), 0)),
            pl.BlockSpec((None, CONV_HALO, 2 * D_WIDTH),
                         lambda s, i: (s, jnp.minimum((i + 1) * per, n_halo - 1), 0)),
            _const_spec((1, C_WIDTH)),
            _const_spec((C_WIDTH, C_WIDTH)),
            _const_spec((1, C_WIDTH)),
            _const_spec((CONV_WIDTH, D_WIDTH)),
            _const_spec((1, D_WIDTH)),
            _const_spec((1, D_WIDTH)),
            _const_spec((1, D_WIDTH)),
            _const_spec((C_WIDTH + D_WIDTH, D_MODEL)),
        ],
        out_specs=_tok_spec(D_MODEL),
        scratch_shapes=[pltpu.VMEM((TOKEN_TILE + 2 * CONV_HALO, D_WIDTH), F32),
                        pltpu.VMEM((TOKEN_TILE, D_WIDTH), BF16),
                        pltpu.VMEM((CONV_ROWS, D_WIDTH), F32)],
        compiler_params=_params("parallel", "arbitrary"),
        name="ssm_out",
    )(x, yf, yb, uc, adgd, adgd, adgd, d_skip, w_glu, b_glu, conv_w, conv_b, ln_g, ln_b, w_out)


def _row(v):
    return v.astype(F32).reshape(1, -1)


def _attn_layer(x, ffn1, g_mix, w, j):
    w_in, gain = w['attn_in'][j]
    x1, qa, kva, qb, kb, vb = _pre_attn(x, ffn1, g_mix, w_in, gain)
    ya = _win_gqa(qa, kva, w['sink_a'][j].astype(F32) * LOG2E, w['win_bias'])
    yb = _nbr_attn(qb, kb, vb, w['nbr_bias'][j])
    return x1, (ya, yb, w['w_attn_out'][j])


def _ssm_layer(x, ffn1, g_mix, w, j):
    x1, uc, adgd = _pre_ssm(x, ffn1, g_mix, w['w_ssm_in'][j])
    b_pack, a_pack, c_pack = w['scan'][j]
    yf, yb = _ssm_scan(uc, b_pack, a_pack, c_pack)
    x2 = _ssm_out(x1, yf, yb, uc, adgd,
                  _row(w['d_skip'][j]), w['w_glu_c'][j], _row(w['b_glu_c'][j]),
                  w['conv_w'][j].astype(F32), _row(w['conv_b'][j]), _row(w['ln_g_d'][j]),
                  _row(w['ln_b_d'][j]), w['w_ssm_out'][j])
    return x2, ()


def _trunk(x, p, w):
    b, seq_len, _ = x.shape
    assert seq_len % TOKEN_TILE == 0 and seq_len % SCAN_CHUNK == 0 and b % SCAN_SEQS == 0
    assert seq_len >= WIN_SPAN and seq_len % WIN == 0
    for i in range(DEPTH):
        ffn1 = (_row(w['norm_ffn1'][i]), w['w_ffn1_in'][i], w['w_ffn1_out'][i])
        ffn2 = (_row(w['norm_ffn2'][i]), w['w_ffn2_in'][i], w['w_ffn2_out'][i])
        ple = (_row(w['norm_ple'][i]), w['w_ple_gate'][i], w['w_ple_proj'][i], _row(w['norm_ple_post'][i]))
        layer = _attn_layer if i % 2 == 0 else _ssm_layer
        x, mix = layer(x, ffn1, _row(w['norm_mix'][i]), w, i // 2)
        x = _post(x, mix, ffn2, p, i, ple)
    return x


def kernel(x_prompt, x_sample, p_prompt, p_sample, norm_ffn1, w_ffn1_in, w_ffn1_out, norm_mix, norm_ffn2, w_ffn2_in, w_ffn2_out, norm_ple, w_ple_gate, w_ple_proj, norm_ple_post, w_attn_in, q_gain_a, k_gain_a, sink_a, q_gain_b, k_gain_b, rpb_b, w_attn_out, w_ssm_in, lam_re, lam_im, log_dt, b_re, b_im, c_re, c_im, d_skip, w_glu_c, b_glu_c, conv_w, conv_b, ln_g_d, ln_b_d, w_ssm_out):
    w = dict(norm_ffn1=norm_ffn1, norm_mix=norm_mix, norm_ffn2=norm_ffn2, norm_ple=norm_ple,
             norm_ple_post=norm_ple_post, q_gain_a=q_gain_a, k_gain_a=k_gain_a, sink_a=sink_a,
             q_gain_b=q_gain_b, k_gain_b=k_gain_b, d_skip=d_skip, b_glu_c=b_glu_c, conv_w=conv_w,
             conv_b=conv_b, ln_g_d=ln_g_d, ln_b_d=ln_b_d)
    for name, val in dict(w_ffn1_in=w_ffn1_in, w_ffn1_out=w_ffn1_out, w_ffn2_in=w_ffn2_in,
                          w_ffn2_out=w_ffn2_out, w_ple_gate=w_ple_gate, w_ple_proj=w_ple_proj,
                          w_attn_out=w_attn_out, w_ssm_in=w_ssm_in,
                          w_glu_c=w_glu_c, w_ssm_out=w_ssm_out).items():
        w[name] = val.astype(BF16)
    w['nbr_bias'] = [_nbr_bias_table(rpb_b[j]) for j in range(rpb_b.shape[0])]
    w['win_bias'] = _win_bias_table()
    w['attn_in'] = [_attn_in_weights(w_attn_in[j], q_gain_a[j], k_gain_a[j], q_gain_b[j], k_gain_b[j])
                    for j in range(w_attn_in.shape[0])]
    w['scan'] = [_ssm_scan_weights(lam_re[j], lam_im[j], log_dt[j], b_re[j], b_im[j], c_re[j], c_im[j])
                 for j in range(lam_re.shape[0])]
    return (_trunk(x_prompt, p_prompt, w), _trunk(x_sample, p_sample, w))
```

```python
import functools
import math

import jax
import jax.numpy as jnp
from jax import lax
from jax.experimental import pallas as pl
from jax.experimental.pallas import tpu as pltpu

D_MODEL = 1024
DEPTH = 4
HEAD_DIM = 64
A_HEADS = 8
A_KV_HEADS = 2
A_GROUP = A_HEADS // A_KV_HEADS
WIN = 128
B_HEADS = 8
GRID_W = 64
NA_ROWS = 8
NA_COLS = 16
C_WIDTH = 512
SSM_GROUP_CH = 16
SSM_GROUPS = C_WIDTH // SSM_GROUP_CH
SSM_STATE = 64
D_WIDTH = 512
CONV_WIDTH = 31
D_FF = 2816
PLE_DIM = 256
A_Q = A_HEADS * HEAD_DIM
A_KV = A_KV_HEADS * HEAD_DIM
B_QKV = B_HEADS * HEAD_DIM
ATTN_IN = A_Q + 2 * A_KV + 3 * B_QKV
NEG_INF = -1e30
EPS = 1e-6
LOG2E = math.log2(math.e)

BF16 = jnp.bfloat16
F32 = jnp.float32

VMEM_LIMIT_BYTES = 52 * 1024 * 1024
LANES = 128
SUBLANES = 8
TOKEN_TILE = 512
FF_CHUNK = 256
SCAN_CHUNK = 64
SCAN_SEQS = SUBLANES
NA_QROWS = 8
CONV_HALO = 16
CONV_ROWS = 64


def _params(*sem):
    return pltpu.CompilerParams(dimension_semantics=sem, vmem_limit_bytes=VMEM_LIMIT_BYTES)


def _rms(x, g):
    return x * lax.rsqrt(jnp.mean(x * x, axis=-1, keepdims=True) + EPS) * g


def _dot(a, b):
    return jnp.dot(a, b, preferred_element_type=F32)


def _dot_nt(a, b):
    return lax.dot_general(a, b, (((1,), (1,)), ((), ())), preferred_element_type=F32)


def _const_spec(shape):
    nd = len(shape)
    return pl.BlockSpec(shape, lambda *_: (0,) * nd, pipeline_mode=pl.Buffered(1))


def _ffn_body(x, g_ref, win_ref, wout_ref, acc_ref):
    xn = _rms(x, g_ref[...]).astype(BF16)
    for c in range(D_FF // FF_CHUNK):
        lo = c * FF_CHUNK
        gate = _dot(xn, win_ref[:, lo:lo + FF_CHUNK])
        up = _dot(xn, win_ref[:, D_FF + lo:D_FF + lo + FF_CHUNK])
        act = (gate * jax.nn.sigmoid(gate) * up).astype(BF16)
        part = _dot(act, wout_ref[lo:lo + FF_CHUNK, :])
        if c == 0:
            acc_ref[...] = part
        else:
            acc_ref[...] += part
    return x + 0.5 * acc_ref[...]


def _ple_body(x, p_ref, g1_ref, wg_ref, wp_ref, g2_ref):
    gate = jax.nn.sigmoid(_dot(_rms(x, g1_ref[...]).astype(BF16), wg_ref[...]))
    proj = _dot(p_ref[...].astype(BF16), wp_ref[...])
    return x + gate * _rms(proj, g2_ref[...])


def _tok_spec(width):
    return pl.BlockSpec((None, TOKEN_TILE, width), lambda s, i: (s, i, 0))


def _ffn_specs():
    return [_const_spec((1, D_MODEL)), _const_spec((D_MODEL, 2 * D_FF)), _const_spec((D_FF, D_MODEL))]


def _ple_specs():
    return [_const_spec((1, D_MODEL)), _const_spec((D_MODEL, D_MODEL)), _const_spec((PLE_DIM, D_MODEL)),
            _const_spec((1, D_MODEL))]


def _gelu_tanh(x):
    return 0.5 * x * (1.0 + jnp.tanh(math.sqrt(2.0 / math.pi) * (x + 0.044715 * (x * x * x))))


def _post_kernel(*refs, n_mix):
    x_ref = refs[0]
    mix_refs = refs[1:1 + n_mix]
    ffn_refs = refs[1 + n_mix:4 + n_mix]
    p_ref = refs[4 + n_mix]
    ple_refs = refs[5 + n_mix:9 + n_mix]
    o_ref, acc_ref = refs[9 + n_mix:]
    x = x_ref[...]
    if n_mix:
        ya_ref, yb_ref, w_ref = mix_refs
        half = ya_ref.shape[-1]
        x = x + _dot(ya_ref[...], w_ref[:half, :]) + _dot(yb_ref[...], w_ref[half:, :])
    x = _ffn_body(x, *ffn_refs, acc_ref)
    o_ref[...] = _ple_body(x, p_ref, *ple_refs)


def _post(x, mix, ffn, p, layer, ple):
    b, seq_len, _ = x.shape
    mix_specs = [_tok_spec(mix[0].shape[-1]), _tok_spec(mix[1].shape[-1]), _const_spec(mix[2].shape)] if mix else []
    return pl.pallas_call(
        functools.partial(_post_kernel, n_mix=len(mix)),
        out_shape=jax.ShapeDtypeStruct(x.shape, F32),
        grid=(b, seq_len // TOKEN_TILE),
        in_specs=([_tok_spec(D_MODEL)] + mix_specs + _ffn_specs()
                  + [pl.BlockSpec((None, None, TOKEN_TILE, PLE_DIM), lambda s, i: (layer, s, i, 0))] + _ple_specs()),
        out_specs=_tok_spec(D_MODEL),
        scratch_shapes=[pltpu.VMEM((TOKEN_TILE, D_MODEL), F32)],
        compiler_params=_params("parallel", "parallel"),
        name="post",
    )(x, *mix, *ffn, p, *ple)


ATTN_CHUNK = 512
ATTN_COLS = 5 * ATTN_CHUNK
ATTN_NORM_PIECES = (0, 1, 2, 4, 5, 6, 7)
NORM_PIECE = 256


def _pre_attn_kernel(x_ref, g1_ref, win_ref, wout_ref, g_ref, w_ref, gain_ref,
                     x1_ref, qa_ref, kva_ref, qb_ref, kb_ref, vb_ref, acc_ref):
    x1 = _ffn_body(x_ref[...], g1_ref, win_ref, wout_ref, acc_ref)
    x1_ref[...] = x1
    xn = _rms(x1, g_ref[...]).astype(BF16)
    for c, ref in enumerate((qa_ref, kva_ref, qb_ref, kb_ref, vb_ref)):
        h = _dot(xn, w_ref[:, c * ATTN_CHUNK:(c + 1) * ATTN_CHUNK])
        for half in range(ATTN_CHUNK // NORM_PIECE):
            piece = c * (ATTN_CHUNK // NORM_PIECE) + half
            lanes = slice(half * NORM_PIECE, (half + 1) * NORM_PIECE)
            hp = h[:, lanes]
            if piece in ATTN_NORM_PIECES:
                hp = hp * lax.rsqrt(_head_sumsq(hp) * (1.0 / HEAD_DIM) + EPS) * gain_ref[:, piece * NORM_PIECE:(piece + 1) * NORM_PIECE]
            ref[:, lanes] = hp.astype(BF16)


def _head_sumsq(h):
    upper = lax.broadcasted_iota(jnp.int32, (h.shape[0], LANES), 1) >= HEAD_DIM
    out = []
    for t in range(h.shape[1] // LANES):
        sq = h[:, t * LANES:(t + 1) * LANES]
        sq = sq * sq
        lo = jnp.sum(jnp.where(upper, 0.0, sq), axis=-1, keepdims=True)
        hi = jnp.sum(jnp.where(upper, sq, 0.0), axis=-1, keepdims=True)
        out.append(jnp.where(upper, hi, lo))
    return jnp.concatenate(out, axis=-1)


def _attn_in_weights(w, q_gain_a, k_gain_a, q_gain_b, k_gain_b):
    ka = [w[:, A_Q + h * HEAD_DIM:A_Q + (h + 1) * HEAD_DIM] for h in range(A_KV_HEADS)]
    va = [w[:, A_Q + A_KV + h * HEAD_DIM:A_Q + A_KV + (h + 1) * HEAD_DIM] for h in range(A_KV_HEADS)]
    w2 = jnp.concatenate([w[:, :A_Q]] + [c for h in range(A_KV_HEADS) for c in (ka[h], ka[h])]
                         + [c for h in range(A_KV_HEADS) for c in (va[h], va[h])]
                         + [w[:, A_Q + 2 * A_KV:]], axis=1).astype(BF16)
    scale = HEAD_DIM ** -0.5 * LOG2E
    f = lambda g, reps, s=1.0: jnp.tile(g.astype(F32) * s, reps)
    gain = jnp.concatenate([f(q_gain_a, A_HEADS, scale), f(k_gain_a, 2 * A_KV_HEADS), jnp.ones((2 * A_KV,), F32),
                            f(q_gain_b, B_HEADS, scale), f(k_gain_b, B_HEADS), jnp.ones((B_QKV,), F32)])
    return w2, gain.reshape(1, ATTN_COLS)


def _pre_attn(x, ffn, g, w2, gain):
    b, seq_len, _ = x.shape
    return pl.pallas_call(
        _pre_attn_kernel,
        out_shape=[jax.ShapeDtypeStruct(x.shape, F32)] + [jax.ShapeDtypeStruct((b, seq_len, ATTN_CHUNK), BF16)] * 5,
        grid=(b, seq_len // TOKEN_TILE),
        in_specs=[_tok_spec(D_MODEL)] + _ffn_specs() + [
            _const_spec((1, D_MODEL)),
            _const_spec((D_MODEL, ATTN_COLS)),
            _const_spec((1, ATTN_COLS)),
        ],
        out_specs=[_tok_spec(D_MODEL)] + [_tok_spec(ATTN_CHUNK)] * 5,
        scratch_shapes=[pltpu.VMEM((TOKEN_TILE, D_MODEL), F32)],
        compiler_params=_params("parallel", "parallel"),
        name="pre_attn",
    )(x, *ffn, g, w2, gain)


def _pair_rows(q_tile):
    upper = lax.broadcasted_iota(jnp.int32, q_tile.shape, 1) >= HEAD_DIM
    zero = jnp.zeros_like(q_tile)
    return jnp.concatenate([jnp.where(upper, zero, q_tile), jnp.where(upper, q_tile, zero)], axis=0)


def _unpair_rows(o2):
    r = o2.shape[0] // 2
    upper = lax.broadcasted_iota(jnp.int32, (r, LANES), 1) >= HEAD_DIM
    return jnp.where(upper, o2[r:], o2[:r])


WIN_SPAN = 3 * WIN


WIN_QBLOCKS = 2


def _win_gqa_kernel(sink_ref, qa_ref, kva_ref, bias_ref, o_ref, *, seq_len):
    second = lax.broadcasted_iota(jnp.int32, (2 * WIN, 1), 0) >= WIN
    n_pairs = A_HEADS // 2
    kv_lanes = [slice((j // (A_GROUP // 2)) * LANES, (j // (A_GROUP // 2) + 1) * LANES) for j in range(n_pairs)]
    sinks = [jnp.where(second, sink_ref[2 * j + 1], sink_ref[2 * j]) for j in range(n_pairs)]
    jobs = []
    for blk in range(WIN_QBLOCKS):
        n = pl.program_id(1) * WIN_QBLOCKS + blk
        start = pl.multiple_of(jnp.clip((n - 1) * WIN, 0, seq_len - WIN_SPAN), WIN)
        jobs += [(slice(blk * WIN, (blk + 1) * WIN), start, n - start // WIN, j) for j in range(n_pairs)]
    scores = [_dot_nt(_pair_rows(qa_ref[rows, j * LANES:(j + 1) * LANES]), kva_ref[pl.ds(start, WIN_SPAN), kv_lanes[j]])
              + bias_ref[var, j] for rows, start, var, j in jobs]
    maxes = [jnp.maximum(jnp.max(s, axis=-1, keepdims=True), sinks[job[3]]) for s, job in zip(scores, jobs)]
    es = [jnp.exp2(s - m) for s, m in zip(scores, maxes)]
    o2s = [_dot(e.astype(BF16), kva_ref[pl.ds(start, WIN_SPAN), 2 * A_KV + kv_lanes[j].start:2 * A_KV + kv_lanes[j].stop])
           for e, (rows, start, var, j) in zip(es, jobs)]
    for e, m, o2, (rows, start, var, j) in zip(es, maxes, o2s, jobs):
        denom = jnp.sum(e, axis=-1, keepdims=True) + jnp.exp2(sinks[j] - m)
        o_ref[rows, j * LANES:(j + 1) * LANES] = _unpair_rows(o2 / denom).astype(BF16)


def _win_bias_table():
    off = jnp.arange(3)[:, None, None] * WIN
    dist = jnp.abs(jnp.arange(WIN)[None, :, None] + off - jnp.arange(WIN_SPAN)[None, None, :])
    slopes = jnp.exp2(-8.0 * jnp.arange(1, A_HEADS + 1, dtype=F32) / A_HEADS) * LOG2E
    bias = -slopes[None, :, None, None] * dist[:, None].astype(F32)
    bias = jnp.where((dist <= WIN)[:, None], bias, NEG_INF)
    return bias.reshape(3, A_HEADS // 2, 2 * WIN, WIN_SPAN)


def _win_gqa(qa, kva, sink, bias_tab):
    b, seq_len, _ = qa.shape
    return pl.pallas_call(
        functools.partial(_win_gqa_kernel, seq_len=seq_len),
        out_shape=jax.ShapeDtypeStruct((b, seq_len, A_Q), BF16),
        grid=(b, seq_len // (WIN * WIN_QBLOCKS)),
        in_specs=[
            pl.BlockSpec(memory_space=pltpu.SMEM),
            pl.BlockSpec((None, WIN * WIN_QBLOCKS, A_Q), lambda i, n: (i, n, 0)),
            pl.BlockSpec((None, seq_len, ATTN_CHUNK), lambda i, n: (i, 0, 0)),
            _const_spec((3, A_HEADS // 2, 2 * WIN, WIN_SPAN)),
        ],
        out_specs=pl.BlockSpec((None, WIN * WIN_QBLOCKS, A_Q), lambda i, n: (i, n, 0)),
        compiler_params=_params("parallel", "arbitrary"),
        name="win_gqa",
    )(sink, qa, kva, bias_tab)


def _nbr_attn_kernel(qb_ref, kb_ref, vb_ref, bias_ref, o_ref, *, n_rows):
    m = pl.program_id(1)
    keys = NA_ROWS * GRID_W

    pairs = [slice(pr * LANES, (pr + 1) * LANES) for pr in range(B_HEADS // 2)]

    def key_offset(i):
        r = NA_QROWS * m + i
        rs = jnp.clip(r - NA_ROWS // 2, 0, n_rows - NA_ROWS)
        return pl.multiple_of(rs * GRID_W, GRID_W), r - rs

    def scores(i):
        off, var = key_offset(i)
        return [_dot_nt(_pair_rows(qb_ref[i * GRID_W:(i + 1) * GRID_W, lanes]), kb_ref[pl.ds(off, keys), lanes])
                + bias_ref[pr, var] for pr, lanes in enumerate(pairs)]

    def finish(i, o2s, denoms):
        for lanes, o2, denom in zip(pairs, o2s, denoms):
            o_ref[i * GRID_W:(i + 1) * GRID_W, lanes] = _unpair_rows(o2 / denom).astype(BF16)

    s_cur, pending = scores(0), None
    for i in range(NA_QROWS):
        s_next = scores(i + 1) if i + 1 < NA_QROWS else None
        off, _ = key_offset(i)
        es = [jnp.exp2(s - jnp.max(s, axis=-1, keepdims=True)) for s in s_cur]
        o2s = [_dot(e.astype(BF16), vb_ref[pl.ds(off, keys), lanes]) for e, lanes in zip(es, pairs)]
        denoms = [jnp.sum(e, axis=-1, keepdims=True) for e in es]
        if pending is not None:
            finish(*pending)
        s_cur, pending = s_next, (i, o2s, denoms)
    finish(*pending)


def _nbr_bias_table(rpb):
    keys = NA_ROWS * GRID_W
    z = jnp.pad(rpb.astype(F32), ((0, 0), (0, 0), (0, GRID_W - (2 * NA_COLS - 1)))).reshape(B_HEADS, -1)
    z = jnp.stack([z[:, (NA_ROWS - 1 - v) * GRID_W:(NA_ROWS - 1 - v) * GRID_W + keys] for v in range(NA_ROWS)], axis=1)
    period = keys + GRID_W
    zz = jnp.concatenate([z[..., NA_COLS - 1:], jnp.zeros((B_HEADS, NA_ROWS, period - keys), F32), z[..., :NA_COLS - 1]], axis=-1)
    skew = jnp.tile(zz, (1, 1, GRID_W))[..., :GRID_W * (period - 1)].reshape(B_HEADS, NA_ROWS, GRID_W, period - 1)
    tab = skew[..., :keys]
    qc = jnp.arange(GRID_W)[:, None]
    kc = jnp.arange(keys)[None, :] % GRID_W
    cs = jnp.clip(qc - NA_COLS // 2, 0, GRID_W - NA_COLS)
    ok = (kc >= cs) & (kc < cs + NA_COLS)
    tab = jnp.where(ok[None, None], tab * LOG2E, NEG_INF)
    tab = tab.reshape(B_HEADS // 2, 2, NA_ROWS, GRID_W, keys)
    return jnp.transpose(tab, (0, 2, 1, 3, 4)).reshape(B_HEADS // 2, NA_ROWS, 2 * GRID_W, keys)


def _nbr_attn(qb, kb, vb, bias_tab):
    b, seq_len, _ = qb.shape
    n_rows = seq_len // GRID_W
    assert n_rows >= NA_ROWS and n_rows % NA_QROWS == 0
    qtile = NA_QROWS * GRID_W
    whole = pl.BlockSpec((None, seq_len, B_QKV), lambda s, m: (s, 0, 0))
    return pl.pallas_call(
        functools.partial(_nbr_attn_kernel, n_rows=n_rows),
        out_shape=jax.ShapeDtypeStruct((b, seq_len, B_QKV), BF16),
        grid=(b, n_rows // NA_QROWS),
        in_specs=[
            pl.BlockSpec((None, qtile, B_QKV), lambda s, m: (s, m, 0)),
            whole, whole,
            _const_spec((B_HEADS // 2, NA_ROWS, 2 * GRID_W, NA_ROWS * GRID_W)),
        ],
        out_specs=pl.BlockSpec((None, qtile, B_QKV), lambda s, m: (s, m, 0)),
        compiler_params=_params("parallel", "arbitrary"),
        name="nbr_attn",
    )(qb, kb, vb, bias_tab)


CONV_SHIFT = CONV_HALO - CONV_WIDTH // 2


def _conv_rows(r0, pad_ref, cw_ref, cb_ref, lng_ref, lnb_ref, yd_ref, stage_ref):
    span = CONV_ROWS + SUBLANES
    for lt in range(D_WIDTH // LANES):
        lanes = slice(lt * LANES, (lt + 1) * LANES)
        acc = None
        for b in range(SUBLANES):
            part = None
            for a in range(-(-(CONV_WIDTH + CONV_SHIFT) // SUBLANES)):
                k = SUBLANES * a + b - CONV_SHIFT
                if 0 <= k < CONV_WIDTH:
                    term = cw_ref[k:k + 1, lanes] * pad_ref[r0 + SUBLANES * a:r0 + SUBLANES * a + span, lanes]
                    part = term if part is None else part + term
            part = part[b:b + CONV_ROWS, :]
            acc = part if acc is None else acc + part
        stage_ref[:, lanes] = acc + cb_ref[:, lanes]
    conv = stage_ref[...]
    xc = conv - jnp.mean(conv, axis=-1, keepdims=True)
    y = xc * lax.rsqrt(jnp.mean(xc * xc, axis=-1, keepdims=True) + EPS) * lng_ref[...] + lnb_ref[...]
    yd_ref[r0:r0 + CONV_ROWS, :] = (y * jax.nn.sigmoid(y)).astype(BF16)


def _pre_ssm_kernel(x_ref, g1_ref, win_ref, wout_ref, g_ref, w_ref, x1_ref, uc_ref, adgd_ref, acc_ref):
    x1 = _ffn_body(x_ref[...], g1_ref, win_ref, wout_ref, acc_ref)
    x1_ref[...] = x1
    xn = _rms(x1, g_ref[...]).astype(BF16)
    uc_ref[...] = _dot(xn, w_ref[:, :C_WIDTH])
    adgd_ref[...] = _dot(xn, w_ref[:, C_WIDTH:])


def _pre_ssm(x, ffn, g, w):
    b, seq_len, _ = x.shape
    widths = (D_MODEL, C_WIDTH, 2 * D_WIDTH)
    return pl.pallas_call(
        _pre_ssm_kernel,
        out_shape=[jax.ShapeDtypeStruct((b, seq_len, n), F32) for n in widths],
        grid=(b, seq_len // TOKEN_TILE),
        in_specs=[_tok_spec(D_MODEL)] + _ffn_specs() + [
            _const_spec((1, D_MODEL)),
            _const_spec((D_MODEL, C_WIDTH + 2 * D_WIDTH)),
        ],
        out_specs=[_tok_spec(n) for n in widths],
        scratch_shapes=[pltpu.VMEM((TOKEN_TILE, D_MODEL), F32)],
        compiler_params=_params("parallel", "parallel"),
        name="pre_ssm",
    )(x, *ffn, g, w)


SLAB_GROUPS = LANES // SSM_GROUP_CH
N_SLABS = SSM_GROUPS // SLAB_GROUPS
SLAB_HALF = SLAB_GROUPS * SSM_STATE
SLAB = 2 * SLAB_HALF
STATE_W = N_SLABS * SLAB


def _ssm_scan_kernel(uf_ref, ub_ref, perm_ref, permt_ref, b_ref, a_ref, c_ref, yf_ref, yb_ref, s_ref, h_ref, yt_ref):
    c = pl.program_id(1)
    rows = SCAN_CHUNK * SCAN_SEQS

    @pl.when(c == 0)
    def _():
        h_ref[...] = jnp.zeros_like(h_ref)

    for d, u_ref in enumerate((uf_ref, ub_ref)):
        u2d = _dot(perm_ref[...], u_ref[...].reshape(rows, C_WIDTH).astype(BF16)).astype(BF16)
        for j in range(N_SLABS):
            bu = _dot(u2d[:, j * LANES:(j + 1) * LANES], b_ref[d, j])
            s_ref[d, :, :, j * SLAB:(j + 1) * SLAB] = bu.reshape(SCAN_CHUNK, SCAN_SEQS, SLAB)

    for pair in range(N_SLABS // 2):
        slabs = (2 * pair, 2 * pair + 1)

        def step(i, carry):
            new = []
            for d in range(2):
                t = i if d == 0 else SCAN_CHUNK - 1 - i
                for idx, j in enumerate(slabs):
                    hr, hi = carry[2 * (2 * d + idx)], carry[2 * (2 * d + idx) + 1]
                    re = slice(j * SLAB, j * SLAB + SLAB_HALF)
                    im = slice(j * SLAB + SLAB_HALF, (j + 1) * SLAB)
                    ar = a_ref[d, :, re]
                    ai = a_ref[d, :, im]
                    nr = ar * hr - ai * hi + s_ref[d, t, :, re]
                    ni = ar * hi + ai * hr + s_ref[d, t, :, im]
                    s_ref[d, t, :, re] = nr
                    s_ref[d, t, :, im] = ni
                    new += [nr, ni]
            return tuple(new)

        init = []
        for d in range(2):
            for j in slabs:
                init += [h_ref[d, :, j * SLAB:j * SLAB + SLAB_HALF],
                         h_ref[d, :, j * SLAB + SLAB_HALF:(j + 1) * SLAB]]
        final = lax.fori_loop(0, SCAN_CHUNK, step, tuple(init), unroll=True)
        k = 0
        for d in range(2):
            for j in slabs:
                h_ref[d, :, j * SLAB:j * SLAB + SLAB_HALF] = final[k]
                h_ref[d, :, j * SLAB + SLAB_HALF:(j + 1) * SLAB] = final[k + 1]
                k += 2

    for d, y_ref in enumerate((yf_ref, yb_ref)):
        for j in range(N_SLABS):
            hs = s_ref[d, :, :, j * SLAB:(j + 1) * SLAB].reshape(rows, SLAB).astype(BF16)
            yt_ref[:, j * LANES:(j + 1) * LANES] = _dot(hs, c_ref[d, j])
        y_sb = _dot(permt_ref[...], yt_ref[...].astype(BF16))
        y_ref[...] = y_sb.astype(BF16).reshape(SCAN_SEQS, SCAN_CHUNK, C_WIDTH)


def _ssm_scan_weights(lam_re, lam_im, log_dt, b_re, b_im, c_re, c_im):
    lam = lax.complex(lam_re.astype(F32), lam_im.astype(F32))
    dt = jnp.exp(log_dt.astype(F32))[:, :, None]
    lam_bar = jnp.exp(lam * dt)
    b_bar = ((lam_bar - 1.0) / lam)[..., None] * lax.complex(b_re.astype(F32), b_im.astype(F32))
    eye = jnp.eye(SLAB_GROUPS, dtype=F32)

    def pack_b(part):
        x = part.reshape(2, N_SLABS, SLAB_GROUPS, SSM_STATE, SSM_GROUP_CH)
        x = jnp.einsum('dsgnp,gh->dsgphn', x, eye)
        return x.reshape(2, N_SLABS, LANES, SLAB_HALF)

    b_pack = jnp.concatenate([pack_b(jnp.real(b_bar)), pack_b(jnp.imag(b_bar))], axis=-1).astype(BF16)

    def pack_a(part):
        return part.reshape(2, N_SLABS, SLAB_HALF)

    a_pack = jnp.concatenate([pack_a(jnp.real(lam_bar)), pack_a(jnp.imag(lam_bar))], axis=-1)
    a_pack = jnp.broadcast_to(a_pack.reshape(2, 1, STATE_W), (2, SCAN_SEQS, STATE_W))

    def pack_c(part):
        x = part.astype(F32).reshape(2, N_SLABS, SLAB_GROUPS, SSM_GROUP_CH, SSM_STATE)
        x = jnp.einsum('dsgpn,gh->dsgnhp', x, eye)
        return x.reshape(2, N_SLABS, SLAB_HALF, LANES)

    c_pack = jnp.concatenate([pack_c(c_re), -pack_c(c_im)], axis=2).astype(BF16)
    return b_pack, a_pack, c_pack


def _ssm_scan(u, b_pack, a_pack, c_pack):
    b, seq_len, _ = u.shape
    nc = seq_len // SCAN_CHUNK
    rows = SCAN_CHUNK * SCAN_SEQS
    blk = (SCAN_SEQS, SCAN_CHUNK, C_WIDTH)
    r = jnp.arange(rows)
    perm = (r[:, None] == (r[None, :] % SCAN_CHUNK) * SCAN_SEQS + r[None, :] // SCAN_CHUNK).astype(BF16)
    return pl.pallas_call(
        _ssm_scan_kernel,
        out_shape=[jax.ShapeDtypeStruct(u.shape, BF16)] * 2,
        grid=(b // SCAN_SEQS, nc),
        in_specs=[
            pl.BlockSpec(blk, lambda s, c: (s, c, 0)),
            pl.BlockSpec(blk, lambda s, c: (s, nc - 1 - c, 0)),
            _const_spec((rows, rows)),
            _const_spec((rows, rows)),
            _const_spec((2, N_SLABS, LANES, SLAB)),
            _const_spec((2, SCAN_SEQS, STATE_W)),
            _const_spec((2, N_SLABS, SLAB, LANES)),
        ],
        out_specs=[
            pl.BlockSpec(blk, lambda s, c: (s, c, 0)),
            pl.BlockSpec(blk, lambda s, c: (s, nc - 1 - c, 0)),
        ],
        scratch_shapes=[pltpu.VMEM((2, SCAN_CHUNK, SCAN_SEQS, STATE_W), F32),
                        pltpu.VMEM((2, SCAN_SEQS, STATE_W), F32),
                        pltpu.VMEM((rows, C_WIDTH), F32)],
        compiler_params=_params("parallel", "arbitrary"),
        name="ssm_scan",
    )(u, u, perm, perm.T, b_pack, a_pack, c_pack)


def _ssm_out_kernel(x_ref, yf_ref, yb_ref, uc_ref, cur_ref, prev_ref, next_ref,
                    dskip_ref, wglu_ref, bglu_ref, cw_ref, cb_ref, lng_ref, lnb_ref, w_ref,
                    o_ref, pad_ref, yd_ref, stage_ref):
    i = pl.program_id(1)
    last = pl.num_programs(1) - 1
    tile = cur_ref.shape[0]

    z = _gelu_tanh(yf_ref[...].astype(F32) + yb_ref[...].astype(F32) + dskip_ref[...] * uc_ref[...])
    yc = z * jax.nn.sigmoid(_dot(z.astype(BF16), wglu_ref[...]) + bglu_ref[...])

    def glu(ref):
        v = ref[...]
        return v[:, :D_WIDTH] * jax.nn.sigmoid(v[:, D_WIDTH:])

    pad_ref[0:CONV_HALO, :] = jnp.where(i > 0, glu(prev_ref), 0.0)
    pad_ref[CONV_HALO:CONV_HALO + tile, :] = glu(cur_ref)
    pad_ref[CONV_HALO + tile:, :] = jnp.where(i < last, glu(next_ref), 0.0)
    for r0 in range(0, tile, CONV_ROWS):
        _conv_rows(r0, pad_ref, cw_ref, cb_ref, lng_ref, lnb_ref, yd_ref, stage_ref)

    o_ref[...] = (x_ref[...]
                  + _dot(yc.astype(BF16), w_ref[:C_WIDTH, :])
                  + _dot(yd_ref[...], w_ref[C_WIDTH:, :]))


def _ssm_out(x, yf, yb, uc, adgd, d_skip, w_glu, b_glu, conv_w, conv_b, ln_g, ln_b, w_out):
    b, seq_len, _ = x.shape
    per = TOKEN_TILE // CONV_HALO
    n_halo = seq_len // CONV_HALO
    return pl.pallas_call(
        _ssm_out_kernel,
        out_shape=jax.ShapeDtypeStruct(x.shape, F32),
        grid=(b, seq_len // TOKEN_TILE),
        in_specs=[
            _tok_spec(D_MODEL), _tok_spec(C_WIDTH), _tok_spec(C_WIDTH), _tok_spec(C_WIDTH), _tok_spec(2 * D_WIDTH),
            pl.BlockSpec((None, CONV_HALO, 2 * D_WIDTH), lambda s, i: (s, jnp.maximum(i * per - 1, 0), 0)),
            pl.BlockSpec((None, CONV_HALO, 2 * D_WIDTH),
                         lambda s, i: (s, jnp.minimum((i + 1) * per, n_halo - 1), 0)),
            _const_spec((1, C_WIDTH)),
            _const_spec((C_WIDTH, C_WIDTH)),
            _const_spec((1, C_WIDTH)),
            _const_spec((CONV_WIDTH, D_WIDTH)),
            _const_spec((1, D_WIDTH)),
            _const_spec((1, D_WIDTH)),
            _const_spec((1, D_WIDTH)),
            _const_spec((C_WIDTH + D_WIDTH, D_MODEL)),
        ],
        out_specs=_tok_spec(D_MODEL),
        scratch_shapes=[pltpu.VMEM((TOKEN_TILE + 2 * CONV_HALO, D_WIDTH), F32),
                        pltpu.VMEM((TOKEN_TILE, D_WIDTH), BF16),
                        pltpu.VMEM((CONV_ROWS, D_WIDTH), F32)],
        compiler_params=_params("parallel", "arbitrary"),
        name="ssm_out",
    )(x, yf, yb, uc, adgd, adgd, adgd, d_skip, w_glu, b_glu, conv_w, conv_b, ln_g, ln_b, w_out)


def _row(v):
    return v.astype(F32).reshape(1, -1)


def _attn_layer(x, ffn1, g_mix, w, j):
    w_in, gain = w['attn_in'][j]
    x1, qa, kva, qb, kb, vb = _pre_attn(x, ffn1, g_mix, w_in, gain)
    ya = _win_gqa(qa, kva, w['sink_a'][j].astype(F32) * LOG2E, w['win_bias'])
    yb = _nbr_attn(qb, kb, vb, w['nbr_bias'][j])
    return x1, (ya, yb, w['w_attn_out'][j])


def _ssm_layer(x, ffn1, g_mix, w, j):
    x1, uc, adgd = _pre_ssm(x, ffn1, g_mix, w['w_ssm_in'][j])
    b_pack, a_pack, c_pack = w['scan'][j]
    yf, yb = _ssm_scan(uc, b_pack, a_pack, c_pack)
    x2 = _ssm_out(x1, yf, yb, uc, adgd,
                  _row(w['d_skip'][j]), w['w_glu_c'][j], _row(w['b_glu_c'][j]),
                  w['conv_w'][j].astype(F32), _row(w['conv_b'][j]), _row(w['ln_g_d'][j]),
                  _row(w['ln_b_d'][j]), w['w_ssm_out'][j])
    return x2, ()


def _trunk(x, p, w):
    b, seq_len, _ = x.shape
    assert seq_len % TOKEN_TILE == 0 and seq_len % SCAN_CHUNK == 0 and b % SCAN_SEQS == 0
    assert seq_len >= WIN_SPAN and seq_len % (WIN * WIN_QBLOCKS) == 0
    for i in range(DEPTH):
        ffn1 = (_row(w['norm_ffn1'][i]), w['w_ffn1_in'][i], w['w_ffn1_out'][i])
        ffn2 = (_row(w['norm_ffn2'][i]), w['w_ffn2_in'][i], w['w_ffn2_out'][i])
        ple = (_row(w['norm_ple'][i]), w['w_ple_gate'][i], w['w_ple_proj'][i], _row(w['norm_ple_post'][i]))
        layer = _attn_layer if i % 2 == 0 else _ssm_layer
        x, mix = layer(x, ffn1, _row(w['norm_mix'][i]), w, i // 2)
        x = _post(x, mix, ffn2, p, i, ple)
    return x


def kernel(x_prompt, x_sample, p_prompt, p_sample, norm_ffn1, w_ffn1_in, w_ffn1_out, norm_mix, norm_ffn2, w_ffn2_in, w_ffn2_out, norm_ple, w_ple_gate, w_ple_proj, norm_ple_post, w_attn_in, q_gain_a, k_gain_a, sink_a, q_gain_b, k_gain_b, rpb_b, w_attn_out, w_ssm_in, lam_re, lam_im, log_dt, b_re, b_im, c_re, c_im, d_skip, w_glu_c, b_glu_c, conv_w, conv_b, ln_g_d, ln_b_d, w_ssm_out):
    w = dict(norm_ffn1=norm_ffn1, norm_mix=norm_mix, norm_ffn2=norm_ffn2, norm_ple=norm_ple,
             norm_ple_post=norm_ple_post, q_gain_a=q_gain_a, k_gain_a=k_gain_a, sink_a=sink_a,
             q_gain_b=q_gain_b, k_gain_b=k_gain_b, d_skip=d_skip, b_glu_c=b_glu_c, conv_w=conv_w,
             conv_b=conv_b, ln_g_d=ln_g_d, ln_b_d=ln_b_d)
    for name, val in dict(w_ffn1_in=w_ffn1_in, w_ffn1_out=w_ffn1_out, w_ffn2_in=w_ffn2_in,
                          w_ffn2_out=w_ffn2_out, w_ple_gate=w_ple_gate, w_ple_proj=w_ple_proj,
                          w_attn_out=w_attn_out, w_ssm_in=w_ssm_in,
                          w_glu_c=w_glu_c, w_ssm_out=w_ssm_out).items():
        w[name] = val.astype(BF16)
    w['nbr_bias'] = [_nbr_bias_table(rpb_b[j]) for j in range(rpb_b.shape[0])]
    w['win_bias'] = _win_bias_table()
    w['attn_in'] = [_attn_in_weights(w_attn_in[j], q_gain_a[j], k_gain_a[j], q_gain_b[j], k_gain_b[j])
                    for j in range(w_attn_in.shape[0])]
    w['scan'] = [_ssm_scan_weights(lam_re[j], lam_im[j], log_dt[j], b_re[j], b_im[j], c_re[j], c_im[j])
                 for j in range(lam_re.shape[0])]
    return (_trunk(x_prompt, p_prompt, w), _trunk(x_sample, p_sample, w))
```

```python
import functools
import math

import jax
import jax.numpy as jnp
from jax import lax
from jax.experimental import pallas as pl
from jax.experimental.pallas import tpu as pltpu

D_MODEL = 1024
DEPTH = 4
HEAD_DIM = 64
A_HEADS = 8
A_KV_HEADS = 2
A_GROUP = A_HEADS // A_KV_HEADS
WIN = 128
B_HEADS = 8
GRID_W = 64
NA_ROWS = 8
NA_COLS = 16
C_WIDTH = 512
SSM_GROUP_CH = 16
SSM_GROUPS = C_WIDTH // SSM_GROUP_CH
SSM_STATE = 64
D_WIDTH = 512
CONV_WIDTH = 31
D_FF = 2816
PLE_DIM = 256
A_Q = A_HEADS * HEAD_DIM
A_KV = A_KV_HEADS * HEAD_DIM
B_QKV = B_HEADS * HEAD_DIM
ATTN_IN = A_Q + 2 * A_KV + 3 * B_QKV
NEG_INF = -1e30
EPS = 1e-6
LOG2E = math.log2(math.e)

BF16 = jnp.bfloat16
F32 = jnp.float32

VMEM_LIMIT_BYTES = 52 * 1024 * 1024
LANES = 128
SUBLANES = 8
TOKEN_TILE = 512
FF_CHUNK = 256
SCAN_CHUNK = 64
SCAN_SEQS = SUBLANES
NA_QROWS = 16
SSM_OUT_TILE = 1024
CONV_HALO = 16
CONV_ROWS = 64


def _params(*sem):
    return pltpu.CompilerParams(dimension_semantics=sem, vmem_limit_bytes=VMEM_LIMIT_BYTES)


def _rms(x, g):
    return x * lax.rsqrt(jnp.mean(x * x, axis=-1, keepdims=True) + EPS) * g


def _dot(a, b):
    return jnp.dot(a, b, preferred_element_type=F32)


def _dot_nt(a, b):
    return lax.dot_general(a, b, (((1,), (1,)), ((), ())), preferred_element_type=F32)


def _const_spec(shape):
    nd = len(shape)
    return pl.BlockSpec(shape, lambda *_: (0,) * nd, pipeline_mode=pl.Buffered(1))


def _ffn_body(x, g_ref, win_ref, wout_ref, acc_ref):
    xn = _rms(x, g_ref[...]).astype(BF16)
    for c in range(D_FF // FF_CHUNK):
        lo = c * FF_CHUNK
        gate = _dot(xn, win_ref[:, lo:lo + FF_CHUNK])
        up = _dot(xn, win_ref[:, D_FF + lo:D_FF + lo + FF_CHUNK])
        act = (gate * jax.nn.sigmoid(gate) * up).astype(BF16)
        part = _dot(act, wout_ref[lo:lo + FF_CHUNK, :])
        if c == 0:
            acc_ref[...] = part
        else:
            acc_ref[...] += part
    return x + 0.5 * acc_ref[...]


def _ple_body(x, p_ref, g1_ref, wg_ref, wp_ref, g2_ref):
    gate = jax.nn.sigmoid(_dot(_rms(x, g1_ref[...]).astype(BF16), wg_ref[...]))
    proj = _dot(p_ref[...].astype(BF16), wp_ref[...])
    return x + gate * _rms(proj, g2_ref[...])


def _tok_spec(width):
    return pl.BlockSpec((None, TOKEN_TILE, width), lambda s, i: (s, i, 0))


def _ffn_specs():
    return [_const_spec((1, D_MODEL)), _const_spec((D_MODEL, 2 * D_FF)), _const_spec((D_FF, D_MODEL))]


def _ple_specs():
    return [_const_spec((1, D_MODEL)), _const_spec((D_MODEL, D_MODEL)), _const_spec((PLE_DIM, D_MODEL)),
            _const_spec((1, D_MODEL))]


def _gelu_tanh(x):
    return 0.5 * x * (1.0 + jnp.tanh(math.sqrt(2.0 / math.pi) * (x + 0.044715 * (x * x * x))))


def _post_kernel(*refs, n_mix):
    x_ref = refs[0]
    mix_refs = refs[1:1 + n_mix]
    ffn_refs = refs[1 + n_mix:4 + n_mix]
    p_ref = refs[4 + n_mix]
    ple_refs = refs[5 + n_mix:9 + n_mix]
    o_ref, acc_ref = refs[9 + n_mix:]
    x = x_ref[...]
    if n_mix:
        ya_ref, yb_ref, w_ref = mix_refs
        half = ya_ref.shape[-1]
        x = x + _dot(ya_ref[...], w_ref[:half, :]) + _dot(yb_ref[...], w_ref[half:, :])
    x = _ffn_body(x, *ffn_refs, acc_ref)
    o_ref[...] = _ple_body(x, p_ref, *ple_refs)


def _post(x, mix, ffn, p, layer, ple):
    b, seq_len, _ = x.shape
    mix_specs = [_tok_spec(mix[0].shape[-1]), _tok_spec(mix[1].shape[-1]), _const_spec(mix[2].shape)] if mix else []
    return pl.pallas_call(
        functools.partial(_post_kernel, n_mix=len(mix)),
        out_shape=jax.ShapeDtypeStruct(x.shape, F32),
        grid=(b, seq_len // TOKEN_TILE),
        in_specs=([_tok_spec(D_MODEL)] + mix_specs + _ffn_specs()
                  + [pl.BlockSpec((None, None, TOKEN_TILE, PLE_DIM), lambda s, i: (layer, s, i, 0))] + _ple_specs()),
        out_specs=_tok_spec(D_MODEL),
        scratch_shapes=[pltpu.VMEM((TOKEN_TILE, D_MODEL), F32)],
        compiler_params=_params("parallel", "parallel"),
        name="post",
    )(x, *mix, *ffn, p, *ple)


ATTN_CHUNK = 512
ATTN_COLS = 5 * ATTN_CHUNK
ATTN_NORM_PIECES = (0, 1, 2, 4, 5, 6, 7)
NORM_PIECE = 256


def _pre_attn_kernel(x_ref, g1_ref, win_ref, wout_ref, g_ref, w_ref, gain_ref,
                     x1_ref, qa_ref, kva_ref, qb_ref, kb_ref, vb_ref, acc_ref):
    x1 = _ffn_body(x_ref[...], g1_ref, win_ref, wout_ref, acc_ref)
    x1_ref[...] = x1
    xn = _rms(x1, g_ref[...]).astype(BF16)
    for c, ref in enumerate((qa_ref, kva_ref, qb_ref, kb_ref, vb_ref)):
        h = _dot(xn, w_ref[:, c * ATTN_CHUNK:(c + 1) * ATTN_CHUNK])
        for half in range(ATTN_CHUNK // NORM_PIECE):
            piece = c * (ATTN_CHUNK // NORM_PIECE) + half
            lanes = slice(half * NORM_PIECE, (half + 1) * NORM_PIECE)
            hp = h[:, lanes]
            if piece in ATTN_NORM_PIECES:
                hp = hp * lax.rsqrt(_head_sumsq(hp) * (1.0 / HEAD_DIM) + EPS) * gain_ref[:, piece * NORM_PIECE:(piece + 1) * NORM_PIECE]
            ref[:, lanes] = hp.astype(BF16)


def _head_sumsq(h):
    upper = lax.broadcasted_iota(jnp.int32, (h.shape[0], LANES), 1) >= HEAD_DIM
    out = []
    for t in range(h.shape[1] // LANES):
        sq = h[:, t * LANES:(t + 1) * LANES]
        sq = sq * sq
        lo = jnp.sum(jnp.where(upper, 0.0, sq), axis=-1, keepdims=True)
        hi = jnp.sum(jnp.where(upper, sq, 0.0), axis=-1, keepdims=True)
        out.append(jnp.where(upper, hi, lo))
    return jnp.concatenate(out, axis=-1)


def _attn_in_weights(w, q_gain_a, k_gain_a, q_gain_b, k_gain_b):
    ka = [w[:, A_Q + h * HEAD_DIM:A_Q + (h + 1) * HEAD_DIM] for h in range(A_KV_HEADS)]
    va = [w[:, A_Q + A_KV + h * HEAD_DIM:A_Q + A_KV + (h + 1) * HEAD_DIM] for h in range(A_KV_HEADS)]
    w2 = jnp.concatenate([w[:, :A_Q]] + [c for h in range(A_KV_HEADS) for c in (ka[h], ka[h])]
                         + [c for h in range(A_KV_HEADS) for c in (va[h], va[h])]
                         + [w[:, A_Q + 2 * A_KV:]], axis=1).astype(BF16)
    scale = HEAD_DIM ** -0.5 * LOG2E
    f = lambda g, reps, s=1.0: jnp.tile(g.astype(F32) * s, reps)
    gain = jnp.concatenate([f(q_gain_a, A_HEADS, scale), f(k_gain_a, 2 * A_KV_HEADS), jnp.ones((2 * A_KV,), F32),
                            f(q_gain_b, B_HEADS, scale), f(k_gain_b, B_HEADS), jnp.ones((B_QKV,), F32)])
    return w2, gain.reshape(1, ATTN_COLS)


def _pre_attn(x, ffn, g, w2, gain):
    b, seq_len, _ = x.shape
    return pl.pallas_call(
        _pre_attn_kernel,
        out_shape=[jax.ShapeDtypeStruct(x.shape, F32)] + [jax.ShapeDtypeStruct((b, seq_len, ATTN_CHUNK), BF16)] * 5,
        grid=(b, seq_len // TOKEN_TILE),
        in_specs=[_tok_spec(D_MODEL)] + _ffn_specs() + [
            _const_spec((1, D_MODEL)),
            _const_spec((D_MODEL, ATTN_COLS)),
            _const_spec((1, ATTN_COLS)),
        ],
        out_specs=[_tok_spec(D_MODEL)] + [_tok_spec(ATTN_CHUNK)] * 5,
        scratch_shapes=[pltpu.VMEM((TOKEN_TILE, D_MODEL), F32)],
        compiler_params=_params("parallel", "parallel"),
        name="pre_attn",
    )(x, *ffn, g, w2, gain)


def _pair_rows(q_tile):
    upper = lax.broadcasted_iota(jnp.int32, q_tile.shape, 1) >= HEAD_DIM
    zero = jnp.zeros_like(q_tile)
    return jnp.concatenate([jnp.where(upper, zero, q_tile), jnp.where(upper, q_tile, zero)], axis=0)


def _unpair_rows(o2):
    r = o2.shape[0] // 2
    upper = lax.broadcasted_iota(jnp.int32, (r, LANES), 1) >= HEAD_DIM
    return jnp.where(upper, o2[r:], o2[:r])


WIN_SPAN = 3 * WIN


WIN_QBLOCKS = 2


def _win_gqa_kernel(sink_ref, qa_ref, kva_ref, bias_ref, o_ref, *, seq_len):
    second = lax.broadcasted_iota(jnp.int32, (2 * WIN, 1), 0) >= WIN
    n_pairs = A_HEADS // 2
    kv_lanes = [slice((j // (A_GROUP // 2)) * LANES, (j // (A_GROUP // 2) + 1) * LANES) for j in range(n_pairs)]
    sinks = [jnp.where(second, sink_ref[2 * j + 1], sink_ref[2 * j]) for j in range(n_pairs)]
    jobs = []
    for blk in range(WIN_QBLOCKS):
        n = pl.program_id(1) * WIN_QBLOCKS + blk
        start = pl.multiple_of(jnp.clip((n - 1) * WIN, 0, seq_len - WIN_SPAN), WIN)
        jobs += [(slice(blk * WIN, (blk + 1) * WIN), start, n - start // WIN, j) for j in range(n_pairs)]
    scores = [_dot_nt(_pair_rows(qa_ref[rows, j * LANES:(j + 1) * LANES]), kva_ref[pl.ds(start, WIN_SPAN), kv_lanes[j]])
              + bias_ref[var, j] for rows, start, var, j in jobs]
    maxes = [jnp.maximum(jnp.max(s, axis=-1, keepdims=True), sinks[job[3]]) for s, job in zip(scores, jobs)]
    es = [jnp.exp2(s - m) for s, m in zip(scores, maxes)]
    o2s = [_dot(e.astype(BF16), kva_ref[pl.ds(start, WIN_SPAN), 2 * A_KV + kv_lanes[j].start:2 * A_KV + kv_lanes[j].stop])
           for e, (rows, start, var, j) in zip(es, jobs)]
    for e, m, o2, (rows, start, var, j) in zip(es, maxes, o2s, jobs):
        denom = jnp.sum(e, axis=-1, keepdims=True) + jnp.exp2(sinks[j] - m)
        o_ref[rows, j * LANES:(j + 1) * LANES] = _unpair_rows(o2 / denom).astype(BF16)


def _win_bias_table():
    off = jnp.arange(3)[:, None, None] * WIN
    dist = jnp.abs(jnp.arange(WIN)[None, :, None] + off - jnp.arange(WIN_SPAN)[None, None, :])
    slopes = jnp.exp2(-8.0 * jnp.arange(1, A_HEADS + 1, dtype=F32) / A_HEADS) * LOG2E
    bias = -slopes[None, :, None, None] * dist[:, None].astype(F32)
    bias = jnp.where((dist <= WIN)[:, None], bias, NEG_INF)
    return bias.reshape(3, A_HEADS // 2, 2 * WIN, WIN_SPAN)


def _win_gqa(qa, kva, sink, bias_tab):
    b, seq_len, _ = qa.shape
    return pl.pallas_call(
        functools.partial(_win_gqa_kernel, seq_len=seq_len),
        out_shape=jax.ShapeDtypeStruct((b, seq_len, A_Q), BF16),
        grid=(b, seq_len // (WIN * WIN_QBLOCKS)),
        in_specs=[
            pl.BlockSpec(memory_space=pltpu.SMEM),
            pl.BlockSpec((None, WIN * WIN_QBLOCKS, A_Q), lambda i, n: (i, n, 0)),
            pl.BlockSpec((None, seq_len, ATTN_CHUNK), lambda i, n: (i, 0, 0)),
            _const_spec((3, A_HEADS // 2, 2 * WIN, WIN_SPAN)),
        ],
        out_specs=pl.BlockSpec((None, WIN * WIN_QBLOCKS, A_Q), lambda i, n: (i, n, 0)),
        compiler_params=_params("parallel", "arbitrary"),
        name="win_gqa",
    )(sink, qa, kva, bias_tab)


def _nbr_attn_kernel(qb_ref, kb_ref, vb_ref, bias_ref, o_ref, *, n_rows):
    m = pl.program_id(1)
    keys = NA_ROWS * GRID_W

    pairs = [slice(pr * LANES, (pr + 1) * LANES) for pr in range(B_HEADS // 2)]

    def key_offset(i):
        r = NA_QROWS * m + i
        rs = jnp.clip(r - NA_ROWS // 2, 0, n_rows - NA_ROWS)
        return pl.multiple_of(rs * GRID_W, GRID_W), r - rs

    def scores(i):
        off, var = key_offset(i)
        return [_dot_nt(_pair_rows(qb_ref[i * GRID_W:(i + 1) * GRID_W, lanes]), kb_ref[pl.ds(off, keys), lanes])
                + bias_ref[pr, var] for pr, lanes in enumerate(pairs)]

    def finish(i, o2s, denoms):
        for lanes, o2, denom in zip(pairs, o2s, denoms):
            o_ref[i * GRID_W:(i + 1) * GRID_W, lanes] = _unpair_rows(o2 / denom).astype(BF16)

    s_cur, pending = scores(0), None
    for i in range(NA_QROWS):
        s_next = scores(i + 1) if i + 1 < NA_QROWS else None
        off, _ = key_offset(i)
        es = [jnp.exp2(s - jnp.max(s, axis=-1, keepdims=True)) for s in s_cur]
        o2s = [_dot(e.astype(BF16), vb_ref[pl.ds(off, keys), lanes]) for e, lanes in zip(es, pairs)]
        denoms = [jnp.sum(e, axis=-1, keepdims=True) for e in es]
        if pending is not None:
            finish(*pending)
        s_cur, pending = s_next, (i, o2s, denoms)
    finish(*pending)


def _nbr_bias_table(rpb):
    keys = NA_ROWS * GRID_W
    z = jnp.pad(rpb.astype(F32), ((0, 0), (0, 0), (0, GRID_W - (2 * NA_COLS - 1)))).reshape(B_HEADS, -1)
    z = jnp.stack([z[:, (NA_ROWS - 1 - v) * GRID_W:(NA_ROWS - 1 - v) * GRID_W + keys] for v in range(NA_ROWS)], axis=1)
    period = keys + GRID_W
    zz = jnp.concatenate([z[..., NA_COLS - 1:], jnp.zeros((B_HEADS, NA_ROWS, period - keys), F32), z[..., :NA_COLS - 1]], axis=-1)
    skew = jnp.tile(zz, (1, 1, GRID_W))[..., :GRID_W * (period - 1)].reshape(B_HEADS, NA_ROWS, GRID_W, period - 1)
    tab = skew[..., :keys]
    qc = jnp.arange(GRID_W)[:, None]
    kc = jnp.arange(keys)[None, :] % GRID_W
    cs = jnp.clip(qc - NA_COLS // 2, 0, GRID_W - NA_COLS)
    ok = (kc >= cs) & (kc < cs + NA_COLS)
    tab = jnp.where(ok[None, None], tab * LOG2E, NEG_INF)
    tab = tab.reshape(B_HEADS // 2, 2, NA_ROWS, GRID_W, keys)
    return jnp.transpose(tab, (0, 2, 1, 3, 4)).reshape(B_HEADS // 2, NA_ROWS, 2 * GRID_W, keys)


def _nbr_attn(qb, kb, vb, bias_tab):
    b, seq_len, _ = qb.shape
    n_rows = seq_len // GRID_W
    assert n_rows >= NA_ROWS and n_rows % NA_QROWS == 0
    qtile = NA_QROWS * GRID_W
    whole = pl.BlockSpec((None, seq_len, B_QKV), lambda s, m: (s, 0, 0))
    return pl.pallas_call(
        functools.partial(_nbr_attn_kernel, n_rows=n_rows),
        out_shape=jax.ShapeDtypeStruct((b, seq_len, B_QKV), BF16),
        grid=(b, n_rows // NA_QROWS),
        in_specs=[
            pl.BlockSpec((None, qtile, B_QKV), lambda s, m: (s, m, 0)),
            whole, whole,
            _const_spec((B_HEADS // 2, NA_ROWS, 2 * GRID_W, NA_ROWS * GRID_W)),
        ],
        out_specs=pl.BlockSpec((None, qtile, B_QKV), lambda s, m: (s, m, 0)),
        compiler_params=_params("parallel", "arbitrary"),
        name="nbr_attn",
    )(qb, kb, vb, bias_tab)


CONV_SHIFT = CONV_HALO - CONV_WIDTH // 2


def _conv_rows(r0, pad_ref, cw_ref, cb_ref, lng_ref, lnb_ref, yd_ref, stage_ref):
    span = CONV_ROWS + SUBLANES
    for lt in range(D_WIDTH // LANES):
        lanes = slice(lt * LANES, (lt + 1) * LANES)
        acc = None
        for b in range(SUBLANES):
            part = None
            for a in range(-(-(CONV_WIDTH + CONV_SHIFT) // SUBLANES)):
                k = SUBLANES * a + b - CONV_SHIFT
                if 0 <= k < CONV_WIDTH:
                    term = cw_ref[k:k + 1, lanes] * pad_ref[r0 + SUBLANES * a:r0 + SUBLANES * a + span, lanes]
                    part = term if part is None else part + term
            part = part[b:b + CONV_ROWS, :]
            acc = part if acc is None else acc + part
        stage_ref[:, lanes] = acc + cb_ref[:, lanes]
    conv = stage_ref[...]
    xc = conv - jnp.mean(conv, axis=-1, keepdims=True)
    y = xc * lax.rsqrt(jnp.mean(xc * xc, axis=-1, keepdims=True) + EPS) * lng_ref[...] + lnb_ref[...]
    yd_ref[r0:r0 + CONV_ROWS, :] = (y * jax.nn.sigmoid(y)).astype(BF16)


def _pre_ssm_kernel(x_ref, g1_ref, win_ref, wout_ref, g_ref, w_ref, x1_ref, uc_ref, adgd_ref, acc_ref):
    x1 = _ffn_body(x_ref[...], g1_ref, win_ref, wout_ref, acc_ref)
    x1_ref[...] = x1
    xn = _rms(x1, g_ref[...]).astype(BF16)
    uc_ref[...] = _dot(xn, w_ref[:, :C_WIDTH])
    adgd_ref[...] = _dot(xn, w_ref[:, C_WIDTH:])


def _pre_ssm(x, ffn, g, w):
    b, seq_len, _ = x.shape
    widths = (D_MODEL, C_WIDTH, 2 * D_WIDTH)
    return pl.pallas_call(
        _pre_ssm_kernel,
        out_shape=[jax.ShapeDtypeStruct((b, seq_len, n), F32) for n in widths],
        grid=(b, seq_len // TOKEN_TILE),
        in_specs=[_tok_spec(D_MODEL)] + _ffn_specs() + [
            _const_spec((1, D_MODEL)),
            _const_spec((D_MODEL, C_WIDTH + 2 * D_WIDTH)),
        ],
        out_specs=[_tok_spec(n) for n in widths],
        scratch_shapes=[pltpu.VMEM((TOKEN_TILE, D_MODEL), F32)],
        compiler_params=_params("parallel", "parallel"),
        name="pre_ssm",
    )(x, *ffn, g, w)


SLAB_GROUPS = LANES // SSM_GROUP_CH
N_SLABS = SSM_GROUPS // SLAB_GROUPS
SLAB_HALF = SLAB_GROUPS * SSM_STATE
SLAB = 2 * SLAB_HALF
STATE_W = N_SLABS * SLAB


def _ssm_scan_kernel(uf_ref, ub_ref, perm_ref, permt_ref, b_ref, a_ref, c_ref, yf_ref, yb_ref, s_ref, h_ref, yt_ref):
    c = pl.program_id(1)
    rows = SCAN_CHUNK * SCAN_SEQS

    @pl.when(c == 0)
    def _():
        h_ref[...] = jnp.zeros_like(h_ref)

    for d, u_ref in enumerate((uf_ref, ub_ref)):
        u2d = _dot(perm_ref[...], u_ref[...].reshape(rows, C_WIDTH).astype(BF16)).astype(BF16)
        for j in range(N_SLABS):
            bu = _dot(u2d[:, j * LANES:(j + 1) * LANES], b_ref[d, j])
            s_ref[d, :, :, j * SLAB:(j + 1) * SLAB] = bu.reshape(SCAN_CHUNK, SCAN_SEQS, SLAB)

    for pair in range(N_SLABS // 2):
        slabs = (2 * pair, 2 * pair + 1)

        def step(i, carry):
            new = []
            for d in range(2):
                t = i if d == 0 else SCAN_CHUNK - 1 - i
                for idx, j in enumerate(slabs):
                    hr, hi = carry[2 * (2 * d + idx)], carry[2 * (2 * d + idx) + 1]
                    re = slice(j * SLAB, j * SLAB + SLAB_HALF)
                    im = slice(j * SLAB + SLAB_HALF, (j + 1) * SLAB)
                    ar = a_ref[d, :, re]
                    ai = a_ref[d, :, im]
                    nr = ar * hr - ai * hi + s_ref[d, t, :, re]
                    ni = ar * hi + ai * hr + s_ref[d, t, :, im]
                    s_ref[d, t, :, re] = nr
                    s_ref[d, t, :, im] = ni
                    new += [nr, ni]
            return tuple(new)

        init = []
        for d in range(2):
            for j in slabs:
                init += [h_ref[d, :, j * SLAB:j * SLAB + SLAB_HALF],
                         h_ref[d, :, j * SLAB + SLAB_HALF:(j + 1) * SLAB]]
        final = lax.fori_loop(0, SCAN_CHUNK, step, tuple(init), unroll=True)
        k = 0
        for d in range(2):
            for j in slabs:
                h_ref[d, :, j * SLAB:j * SLAB + SLAB_HALF] = final[k]
                h_ref[d, :, j * SLAB + SLAB_HALF:(j + 1) * SLAB] = final[k + 1]
                k += 2

    for d, y_ref in enumerate((yf_ref, yb_ref)):
        for j in range(N_SLABS):
            hs = s_ref[d, :, :, j * SLAB:(j + 1) * SLAB].reshape(rows, SLAB).astype(BF16)
            yt_ref[:, j * LANES:(j + 1) * LANES] = _dot(hs, c_ref[d, j])
        y_sb = _dot(permt_ref[...], yt_ref[...].astype(BF16))
        y_ref[...] = y_sb.astype(BF16).reshape(SCAN_SEQS, SCAN_CHUNK, C_WIDTH)


def _ssm_scan_weights(lam_re, lam_im, log_dt, b_re, b_im, c_re, c_im):
    lam = lax.complex(lam_re.astype(F32), lam_im.astype(F32))
    dt = jnp.exp(log_dt.astype(F32))[:, :, None]
    lam_bar = jnp.exp(lam * dt)
    b_bar = ((lam_bar - 1.0) / lam)[..., None] * lax.complex(b_re.astype(F32), b_im.astype(F32))
    eye = jnp.eye(SLAB_GROUPS, dtype=F32)

    def pack_b(part):
        x = part.reshape(2, N_SLABS, SLAB_GROUPS, SSM_STATE, SSM_GROUP_CH)
        x = jnp.einsum('dsgnp,gh->dsgphn', x, eye)
        return x.reshape(2, N_SLABS, LANES, SLAB_HALF)

    b_pack = jnp.concatenate([pack_b(jnp.real(b_bar)), pack_b(jnp.imag(b_bar))], axis=-1).astype(BF16)

    def pack_a(part):
        return part.reshape(2, N_SLABS, SLAB_HALF)

    a_pack = jnp.concatenate([pack_a(jnp.real(lam_bar)), pack_a(jnp.imag(lam_bar))], axis=-1)
    a_pack = jnp.broadcast_to(a_pack.reshape(2, 1, STATE_W), (2, SCAN_SEQS, STATE_W))

    def pack_c(part):
        x = part.astype(F32).reshape(2, N_SLABS, SLAB_GROUPS, SSM_GROUP_CH, SSM_STATE)
        x = jnp.einsum('dsgpn,gh->dsgnhp', x, eye)
        return x.reshape(2, N_SLABS, SLAB_HALF, LANES)

    c_pack = jnp.concatenate([pack_c(c_re), -pack_c(c_im)], axis=2).astype(BF16)
    return b_pack, a_pack, c_pack


def _ssm_scan(u, b_pack, a_pack, c_pack):
    b, seq_len, _ = u.shape
    nc = seq_len // SCAN_CHUNK
    rows = SCAN_CHUNK * SCAN_SEQS
    blk = (SCAN_SEQS, SCAN_CHUNK, C_WIDTH)
    r = jnp.arange(rows)
    perm = (r[:, None] == (r[None, :] % SCAN_CHUNK) * SCAN_SEQS + r[None, :] // SCAN_CHUNK).astype(BF16)
    return pl.pallas_call(
        _ssm_scan_kernel,
        out_shape=[jax.ShapeDtypeStruct(u.shape, BF16)] * 2,
        grid=(b // SCAN_SEQS, nc),
        in_specs=[
            pl.BlockSpec(blk, lambda s, c: (s, c, 0)),
            pl.BlockSpec(blk, lambda s, c: (s, nc - 1 - c, 0)),
            _const_spec((rows, rows)),
            _const_spec((rows, rows)),
            _const_spec((2, N_SLABS, LANES, SLAB)),
            _const_spec((2, SCAN_SEQS, STATE_W)),
            _const_spec((2, N_SLABS, SLAB, LANES)),
        ],
        out_specs=[
            pl.BlockSpec(blk, lambda s, c: (s, c, 0)),
            pl.BlockSpec(blk, lambda s, c: (s, nc - 1 - c, 0)),
        ],
        scratch_shapes=[pltpu.VMEM((2, SCAN_CHUNK, SCAN_SEQS, STATE_W), F32),
                        pltpu.VMEM((2, SCAN_SEQS, STATE_W), F32),
                        pltpu.VMEM((rows, C_WIDTH), F32)],
        compiler_params=_params("parallel", "arbitrary"),
        name="ssm_scan",
    )(u, u, perm, perm.T, b_pack, a_pack, c_pack)


def _ssm_out_kernel(x_ref, yf_ref, yb_ref, uc_ref, cur_ref, prev_ref, next_ref,
                    dskip_ref, wglu_ref, bglu_ref, cw_ref, cb_ref, lng_ref, lnb_ref, w_ref,
                    o_ref, pad_ref, yd_ref, stage_ref):
    i = pl.program_id(1)
    last = pl.num_programs(1) - 1
    tile = cur_ref.shape[0]

    z = _gelu_tanh(yf_ref[...].astype(F32) + yb_ref[...].astype(F32) + dskip_ref[...] * uc_ref[...])
    yc = z * jax.nn.sigmoid(_dot(z.astype(BF16), wglu_ref[...]) + bglu_ref[...])

    def glu(ref):
        v = ref[...]
        return v[:, :D_WIDTH] * jax.nn.sigmoid(v[:, D_WIDTH:])

    pad_ref[0:CONV_HALO, :] = jnp.where(i > 0, glu(prev_ref), 0.0)
    pad_ref[CONV_HALO:CONV_HALO + tile, :] = glu(cur_ref)
    pad_ref[CONV_HALO + tile:, :] = jnp.where(i < last, glu(next_ref), 0.0)
    for r0 in range(0, tile, CONV_ROWS):
        _conv_rows(r0, pad_ref, cw_ref, cb_ref, lng_ref, lnb_ref, yd_ref, stage_ref)

    o_ref[...] = (x_ref[...]
                  + _dot(yc.astype(BF16), w_ref[:C_WIDTH, :])
                  + _dot(yd_ref[...], w_ref[C_WIDTH:, :]))


def _ssm_out(x, yf, yb, uc, adgd, d_skip, w_glu, b_glu, conv_w, conv_b, ln_g, ln_b, w_out):
    b, seq_len, _ = x.shape
    tile = min(SSM_OUT_TILE, seq_len)
    assert seq_len % tile == 0
    per = tile // CONV_HALO
    n_halo = seq_len // CONV_HALO

    def tok(width):
        return pl.BlockSpec((None, tile, width), lambda s, i: (s, i, 0))

    return pl.pallas_call(
        _ssm_out_kernel,
        out_shape=jax.ShapeDtypeStruct(x.shape, F32),
        grid=(b, seq_len // tile),
        in_specs=[
            tok(D_MODEL), tok(C_WIDTH), tok(C_WIDTH), tok(C_WIDTH), tok(2 * D_WIDTH),
            pl.BlockSpec((None, CONV_HALO, 2 * D_WIDTH), lambda s, i: (s, jnp.maximum(i * per - 1, 0), 0)),
            pl.BlockSpec((None, CONV_HALO, 2 * D_WIDTH),
                         lambda s, i: (s, jnp.minimum((i + 1) * per, n_halo - 1), 0)),
            _const_spec((1, C_WIDTH)),
            _const_spec((C_WIDTH, C_WIDTH)),
            _const_spec((1, C_WIDTH)),
            _const_spec((CONV_WIDTH, D_WIDTH)),
            _const_spec((1, D_WIDTH)),
            _const_spec((1, D_WIDTH)),
            _const_spec((1, D_WIDTH)),
            _const_spec((C_WIDTH + D_WIDTH, D_MODEL)),
        ],
        out_specs=tok(D_MODEL),
        scratch_shapes=[pltpu.VMEM((tile + 2 * CONV_HALO, D_WIDTH), F32),
                        pltpu.VMEM((tile, D_WIDTH), BF16),
                        pltpu.VMEM((CONV_ROWS, D_WIDTH), F32)],
        compiler_params=_params("parallel", "arbitrary"),
        name="ssm_out",
    )(x, yf, yb, uc, adgd, adgd, adgd, d_skip, w_glu, b_glu, conv_w, conv_b, ln_g, ln_b, w_out)


def _row(v):
    return v.astype(F32).reshape(1, -1)


def _attn_layer(x, ffn1, g_mix, w, j):
    w_in, gain = w['attn_in'][j]
    x1, qa, kva, qb, kb, vb = _pre_attn(x, ffn1, g_mix, w_in, gain)
    ya = _win_gqa(qa, kva, w['sink_a'][j].astype(F32) * LOG2E, w['win_bias'])
    yb = _nbr_attn(qb, kb, vb, w['nbr_bias'][j])
    return x1, (ya, yb, w['w_attn_out'][j])


def _ssm_layer(x, ffn1, g_mix, w, j):
    x1, uc, adgd = _pre_ssm(x, ffn1, g_mix, w['w_ssm_in'][j])
    b_pack, a_pack, c_pack = w['scan'][j]
    yf, yb = _ssm_scan(uc, b_pack, a_pack, c_pack)
    x2 = _ssm_out(x1, yf, yb, uc, adgd,
                  _row(w['d_skip'][j]), w['w_glu_c'][j], _row(w['b_glu_c'][j]),
                  w['conv_w'][j].astype(F32), _row(w['conv_b'][j]), _row(w['ln_g_d'][j]),
                  _row(w['ln_b_d'][j]), w['w_ssm_out'][j])
    return x2, ()


def _trunk(x, p, w):
    b, seq_len, _ = x.shape
    assert seq_len % TOKEN_TILE == 0 and seq_len % SCAN_CHUNK == 0 and b % SCAN_SEQS == 0
    assert seq_len >= WIN_SPAN and seq_len % (WIN * WIN_QBLOCKS) == 0
    for i in range(DEPTH):
        ffn1 = (_row(w['norm_ffn1'][i]), w['w_ffn1_in'][i], w['w_ffn1_out'][i])
        ffn2 = (_row(w['norm_ffn2'][i]), w['w_ffn2_in'][i], w['w_ffn2_out'][i])
        ple = (_row(w['norm_ple'][i]), w['w_ple_gate'][i], w['w_ple_proj'][i], _row(w['norm_ple_post'][i]))
        layer = _attn_layer if i % 2 == 0 else _ssm_layer
        x, mix = layer(x, ffn1, _row(w['norm_mix'][i]), w, i // 2)
        x = _post(x, mix, ffn2, p, i, ple)
    return x


def kernel(x_prompt, x_sample, p_prompt, p_sample, norm_ffn1, w_ffn1_in, w_ffn1_out, norm_mix, norm_ffn2, w_ffn2_in, w_ffn2_out, norm_ple, w_ple_gate, w_ple_proj, norm_ple_post, w_attn_in, q_gain_a, k_gain_a, sink_a, q_gain_b, k_gain_b, rpb_b, w_attn_out, w_ssm_in, lam_re, lam_im, log_dt, b_re, b_im, c_re, c_im, d_skip, w_glu_c, b_glu_c, conv_w, conv_b, ln_g_d, ln_b_d, w_ssm_out):
    w = dict(norm_ffn1=norm_ffn1, norm_mix=norm_mix, norm_ffn2=norm_ffn2, norm_ple=norm_ple,
             norm_ple_post=norm_ple_post, q_gain_a=q_gain_a, k_gain_a=k_gain_a, sink_a=sink_a,
             q_gain_b=q_gain_b, k_gain_b=k_gain_b, d_skip=d_skip, b_glu_c=b_glu_c, conv_w=conv_w,
             conv_b=conv_b, ln_g_d=ln_g_d, ln_b_d=ln_b_d)
    for name, val in dict(w_ffn1_in=w_ffn1_in, w_ffn1_out=w_ffn1_out, w_ffn2_in=w_ffn2_in,
                          w_ffn2_out=w_ffn2_out, w_ple_gate=w_ple_gate, w_ple_proj=w_ple_proj,
                          w_attn_out=w_attn_out, w_ssm_in=w_ssm_in,
                          w_glu_c=w_glu_c, w_ssm_out=w_ssm_out).items():
        w[name] = val.astype(BF16)
    w['nbr_bias'] = [_nbr_bias_table(rpb_b[j]) for j in range(rpb_b.shape[0])]
    w['win_bias'] = _win_bias_table()
    w['attn_in'] = [_attn_in_weights(w_attn_in[j], q_gain_a[j], k_gain_a[j], q_gain_b[j], k_gain_b[j])
                    for j in range(w_attn_in.shape[0])]
    w['scan'] = [_ssm_scan_weights(lam_re[j], lam_im[j], log_dt[j], b_re[j], b_im[j], c_re[j], c_im[j])
                 for j in range(lam_re.shape[0])]
    return (_trunk(x_prompt, p_prompt, w), _trunk(x_sample, p_sample, w))
```

```python
import functools
import math

import jax
import jax.numpy as jnp
from jax import lax
from jax.experimental import pallas as pl
from jax.experimental.pallas import tpu as pltpu

D_MODEL = 1024
DEPTH = 4
HEAD_DIM = 64
A_HEADS = 8
A_KV_HEADS = 2
A_GROUP = A_HEADS // A_KV_HEADS
WIN = 128
B_HEADS = 8
GRID_W = 64
NA_ROWS = 8
NA_COLS = 16
C_WIDTH = 512
SSM_GROUP_CH = 16
SSM_GROUPS = C_WIDTH // SSM_GROUP_CH
SSM_STATE = 64
D_WIDTH = 512
CONV_WIDTH = 31
D_FF = 2816
PLE_DIM = 256
A_Q = A_HEADS * HEAD_DIM
A_KV = A_KV_HEADS * HEAD_DIM
B_QKV = B_HEADS * HEAD_DIM
ATTN_IN = A_Q + 2 * A_KV + 3 * B_QKV
NEG_INF = -1e30
EPS = 1e-6
LOG2E = math.log2(math.e)

BF16 = jnp.bfloat16
F32 = jnp.float32

VMEM_LIMIT_BYTES = 52 * 1024 * 1024
LANES = 128
SUBLANES = 8
TOKEN_TILE = 512
FF_CHUNK = 256
SCAN_CHUNK = 64
SCAN_SEQS = SUBLANES
NA_QROWS = 16
SSM_OUT_TILE = 1024
CONV_HALO = 16
CONV_ROWS = 64


def _params(*sem):
    return pltpu.CompilerParams(dimension_semantics=sem, vmem_limit_bytes=VMEM_LIMIT_BYTES)


def _rms(x, g):
    return x * lax.rsqrt(jnp.mean(x * x, axis=-1, keepdims=True) + EPS) * g


def _dot(a, b):
    return jnp.dot(a, b, preferred_element_type=F32)


def _dot_nt(a, b):
    return lax.dot_general(a, b, (((1,), (1,)), ((), ())), preferred_element_type=F32)


def _const_spec(shape):
    nd = len(shape)
    return pl.BlockSpec(shape, lambda *_: (0,) * nd, pipeline_mode=pl.Buffered(1))


def _ffn_body(x, g_ref, win_ref, wout_ref, acc_ref):
    xn = _rms(x, g_ref[...]).astype(BF16)
    for c in range(D_FF // FF_CHUNK):
        lo = c * FF_CHUNK
        gate = _dot(xn, win_ref[:, lo:lo + FF_CHUNK])
        up = _dot(xn, win_ref[:, D_FF + lo:D_FF + lo + FF_CHUNK])
        act = (gate * jax.nn.sigmoid(gate) * up).astype(BF16)
        part = _dot(act, wout_ref[lo:lo + FF_CHUNK, :])
        if c == 0:
            acc_ref[...] = part
        else:
            acc_ref[...] += part
    return x + 0.5 * acc_ref[...]


def _ple_body(x, p_ref, g1_ref, wg_ref, wp_ref, g2_ref):
    gate = jax.nn.sigmoid(_dot(_rms(x, g1_ref[...]).astype(BF16), wg_ref[...]))
    proj = _dot(p_ref[...].astype(BF16), wp_ref[...])
    return x + gate * _rms(proj, g2_ref[...])


def _tok_spec(width):
    return pl.BlockSpec((None, TOKEN_TILE, width), lambda s, i: (s, i, 0))


def _ffn_specs():
    return [_const_spec((1, D_MODEL)), _const_spec((D_MODEL, 2 * D_FF)), _const_spec((D_FF, D_MODEL))]


def _ple_specs():
    return [_const_spec((1, D_MODEL)), _const_spec((D_MODEL, D_MODEL)), _const_spec((PLE_DIM, D_MODEL)),
            _const_spec((1, D_MODEL))]


def _gelu_tanh(x):
    return 0.5 * x * (1.0 + jnp.tanh(math.sqrt(2.0 / math.pi) * (x + 0.044715 * (x * x * x))))


def _post_kernel(*refs, n_mix):
    x_ref = refs[0]
    mix_refs = refs[1:1 + n_mix]
    ffn_refs = refs[1 + n_mix:4 + n_mix]
    p_ref = refs[4 + n_mix]
    ple_refs = refs[5 + n_mix:9 + n_mix]
    o_ref, acc_ref = refs[9 + n_mix:]
    x = x_ref[...]
    if n_mix:
        ya_ref, yb_ref, w_ref = mix_refs
        half = ya_ref.shape[-1]
        x = x + _dot(ya_ref[...], w_ref[:half, :]) + _dot(yb_ref[...], w_ref[half:, :])
    x = _ffn_body(x, *ffn_refs, acc_ref)
    o_ref[...] = _ple_body(x, p_ref, *ple_refs)


def _post(x, mix, ffn, p, layer, ple):
    b, seq_len, _ = x.shape
    mix_specs = [_tok_spec(mix[0].shape[-1]), _tok_spec(mix[1].shape[-1]), _const_spec(mix[2].shape)] if mix else []
    return pl.pallas_call(
        functools.partial(_post_kernel, n_mix=len(mix)),
        out_shape=jax.ShapeDtypeStruct(x.shape, F32),
        grid=(b, seq_len // TOKEN_TILE),
        in_specs=([_tok_spec(D_MODEL)] + mix_specs + _ffn_specs()
                  + [pl.BlockSpec((None, None, TOKEN_TILE, PLE_DIM), lambda s, i: (layer, s, i, 0))] + _ple_specs()),
        out_specs=_tok_spec(D_MODEL),
        scratch_shapes=[pltpu.VMEM((TOKEN_TILE, D_MODEL), F32)],
        compiler_params=_params("parallel", "parallel"),
        name="post",
    )(x, *mix, *ffn, p, *ple)


ATTN_CHUNK = 512
ATTN_COLS = 5 * ATTN_CHUNK
ATTN_NORM_PIECES = (0, 1, 2, 4, 5, 6, 7)
NORM_PIECE = 256


def _pre_attn_kernel(x_ref, g1_ref, win_ref, wout_ref, g_ref, w_ref, gain_ref,
                     x1_ref, qa_ref, kva_ref, qb_ref, kb_ref, vb_ref, acc_ref):
    x1 = _ffn_body(x_ref[...], g1_ref, win_ref, wout_ref, acc_ref)
    x1_ref[...] = x1
    xn = _rms(x1, g_ref[...]).astype(BF16)
    for c, ref in enumerate((qa_ref, kva_ref, qb_ref, kb_ref, vb_ref)):
        h = _dot(xn, w_ref[:, c * ATTN_CHUNK:(c + 1) * ATTN_CHUNK])
        for half in range(ATTN_CHUNK // NORM_PIECE):
            piece = c * (ATTN_CHUNK // NORM_PIECE) + half
            lanes = slice(half * NORM_PIECE, (half + 1) * NORM_PIECE)
            hp = h[:, lanes]
            if piece in ATTN_NORM_PIECES:
                hp = hp * lax.rsqrt(_head_sumsq(hp) * (1.0 / HEAD_DIM) + EPS) * gain_ref[:, piece * NORM_PIECE:(piece + 1) * NORM_PIECE]
            ref[:, lanes] = hp.astype(BF16)


def _head_sumsq(h):
    upper = lax.broadcasted_iota(jnp.int32, (h.shape[0], LANES), 1) >= HEAD_DIM
    out = []
    for t in range(h.shape[1] // LANES):
        sq = h[:, t * LANES:(t + 1) * LANES]
        sq = sq * sq
        lo = jnp.sum(jnp.where(upper, 0.0, sq), axis=-1, keepdims=True)
        hi = jnp.sum(jnp.where(upper, sq, 0.0), axis=-1, keepdims=True)
        out.append(jnp.where(upper, hi, lo))
    return jnp.concatenate(out, axis=-1)


def _attn_in_weights(w, q_gain_a, k_gain_a, q_gain_b, k_gain_b):
    ka = [w[:, A_Q + h * HEAD_DIM:A_Q + (h + 1) * HEAD_DIM] for h in range(A_KV_HEADS)]
    va = [w[:, A_Q + A_KV + h * HEAD_DIM:A_Q + A_KV + (h + 1) * HEAD_DIM] for h in range(A_KV_HEADS)]
    w2 = jnp.concatenate([w[:, :A_Q]] + [c for h in range(A_KV_HEADS) for c in (ka[h], ka[h])]
                         + [c for h in range(A_KV_HEADS) for c in (va[h], va[h])]
                         + [w[:, A_Q + 2 * A_KV:]], axis=1).astype(BF16)
    scale = HEAD_DIM ** -0.5 * LOG2E
    f = lambda g, reps, s=1.0: jnp.tile(g.astype(F32) * s, reps)
    gain = jnp.concatenate([f(q_gain_a, A_HEADS, scale), f(k_gain_a, 2 * A_KV_HEADS), jnp.ones((2 * A_KV,), F32),
                            f(q_gain_b, B_HEADS, scale), f(k_gain_b, B_HEADS), jnp.ones((B_QKV,), F32)])
    return w2, gain.reshape(1, ATTN_COLS)


def _pre_attn(x, ffn, g, w2, gain):
    b, seq_len, _ = x.shape
    return pl.pallas_call(
        _pre_attn_kernel,
        out_shape=[jax.ShapeDtypeStruct(x.shape, F32)] + [jax.ShapeDtypeStruct((b, seq_len, ATTN_CHUNK), BF16)] * 5,
        grid=(b, seq_len // TOKEN_TILE),
        in_specs=[_tok_spec(D_MODEL)] + _ffn_specs() + [
            _const_spec((1, D_MODEL)),
            _const_spec((D_MODEL, ATTN_COLS)),
            _const_spec((1, ATTN_COLS)),
        ],
        out_specs=[_tok_spec(D_MODEL)] + [_tok_spec(ATTN_CHUNK)] * 5,
        scratch_shapes=[pltpu.VMEM((TOKEN_TILE, D_MODEL), F32)],
        compiler_params=_params("parallel", "parallel"),
        name="pre_attn",
    )(x, *ffn, g, w2, gain)


def _pair_rows(q_tile):
    upper = lax.broadcasted_iota(jnp.int32, q_tile.shape, 1) >= HEAD_DIM
    zero = jnp.zeros_like(q_tile)
    return jnp.concatenate([jnp.where(upper, zero, q_tile), jnp.where(upper, q_tile, zero)], axis=0)


def _unpair_rows(o2):
    r = o2.shape[0] // 2
    upper = lax.broadcasted_iota(jnp.int32, (r, LANES), 1) >= HEAD_DIM
    return jnp.where(upper, o2[r:], o2[:r])


WIN_SPAN = 3 * WIN


WIN_QBLOCKS = 4


def _win_gqa_kernel(sink_ref, qa_ref, kva_ref, bias_ref, o_ref, *, seq_len):
    second = lax.broadcasted_iota(jnp.int32, (2 * WIN, 1), 0) >= WIN
    n_pairs = A_HEADS // 2
    kv_lanes = [slice((j // (A_GROUP // 2)) * LANES, (j // (A_GROUP // 2) + 1) * LANES) for j in range(n_pairs)]
    sinks = [jnp.where(second, sink_ref[2 * j + 1], sink_ref[2 * j]) for j in range(n_pairs)]
    jobs = []
    for blk in range(WIN_QBLOCKS):
        n = pl.program_id(1) * WIN_QBLOCKS + blk
        start = pl.multiple_of(jnp.clip((n - 1) * WIN, 0, seq_len - WIN_SPAN), WIN)
        jobs += [(slice(blk * WIN, (blk + 1) * WIN), start, n - start // WIN, j) for j in range(n_pairs)]
    scores = [_dot_nt(_pair_rows(qa_ref[rows, j * LANES:(j + 1) * LANES]), kva_ref[pl.ds(start, WIN_SPAN), kv_lanes[j]])
              + bias_ref[var, j] for rows, start, var, j in jobs]
    maxes = [jnp.maximum(jnp.max(s, axis=-1, keepdims=True), sinks[job[3]]) for s, job in zip(scores, jobs)]
    es = [jnp.exp2(s - m) for s, m in zip(scores, maxes)]
    o2s = [_dot(e.astype(BF16), kva_ref[pl.ds(start, WIN_SPAN), 2 * A_KV + kv_lanes[j].start:2 * A_KV + kv_lanes[j].stop])
           for e, (rows, start, var, j) in zip(es, jobs)]
    for e, m, o2, (rows, start, var, j) in zip(es, maxes, o2s, jobs):
        denom = jnp.sum(e, axis=-1, keepdims=True) + jnp.exp2(sinks[j] - m)
        o_ref[rows, j * LANES:(j + 1) * LANES] = _unpair_rows(o2 / denom).astype(BF16)


def _win_bias_table():
    off = jnp.arange(3)[:, None, None] * WIN
    dist = jnp.abs(jnp.arange(WIN)[None, :, None] + off - jnp.arange(WIN_SPAN)[None, None, :])
    slopes = jnp.exp2(-8.0 * jnp.arange(1, A_HEADS + 1, dtype=F32) / A_HEADS) * LOG2E
    bias = -slopes[None, :, None, None] * dist[:, None].astype(F32)
    bias = jnp.where((dist <= WIN)[:, None], bias, NEG_INF)
    return bias.reshape(3, A_HEADS // 2, 2 * WIN, WIN_SPAN)


def _win_gqa(qa, kva, sink, bias_tab):
    b, seq_len, _ = qa.shape
    return pl.pallas_call(
        functools.partial(_win_gqa_kernel, seq_len=seq_len),
        out_shape=jax.ShapeDtypeStruct((b, seq_len, A_Q), BF16),
        grid=(b, seq_len // (WIN * WIN_QBLOCKS)),
        in_specs=[
            pl.BlockSpec(memory_space=pltpu.SMEM),
            pl.BlockSpec((None, WIN * WIN_QBLOCKS, A_Q), lambda i, n: (i, n, 0)),
            pl.BlockSpec((None, seq_len, ATTN_CHUNK), lambda i, n: (i, 0, 0)),
            _const_spec((3, A_HEADS // 2, 2 * WIN, WIN_SPAN)),
        ],
        out_specs=pl.BlockSpec((None, WIN * WIN_QBLOCKS, A_Q), lambda i, n: (i, n, 0)),
        compiler_params=_params("parallel", "arbitrary"),
        name="win_gqa",
    )(sink, qa, kva, bias_tab)


def _nbr_attn_kernel(qb_ref, kb_ref, vb_ref, bias_ref, o_ref, *, n_rows):
    m = pl.program_id(1)
    keys = NA_ROWS * GRID_W

    pairs = [slice(pr * LANES, (pr + 1) * LANES) for pr in range(B_HEADS // 2)]

    def key_offset(i):
        r = NA_QROWS * m + i
        rs = jnp.clip(r - NA_ROWS // 2, 0, n_rows - NA_ROWS)
        return pl.multiple_of(rs * GRID_W, GRID_W), r - rs

    def scores(i):
        off, var = key_offset(i)
        return [_dot_nt(_pair_rows(qb_ref[i * GRID_W:(i + 1) * GRID_W, lanes]), kb_ref[pl.ds(off, keys), lanes])
                + bias_ref[pr, var] for pr, lanes in enumerate(pairs)]

    def finish(i, o2s, denoms):
        for lanes, o2, denom in zip(pairs, o2s, denoms):
            o_ref[i * GRID_W:(i + 1) * GRID_W, lanes] = _unpair_rows(o2 / denom).astype(BF16)

    s_cur, pending = scores(0), None
    for i in range(NA_QROWS):
        s_next = scores(i + 1) if i + 1 < NA_QROWS else None
        off, _ = key_offset(i)
        es = [jnp.exp2(s - jnp.max(s, axis=-1, keepdims=True)) for s in s_cur]
        o2s = [_dot(e.astype(BF16), vb_ref[pl.ds(off, keys), lanes]) for e, lanes in zip(es, pairs)]
        denoms = [jnp.sum(e, axis=-1, keepdims=True) for e in es]
        if pending is not None:
            finish(*pending)
        s_cur, pending = s_next, (i, o2s, denoms)
    finish(*pending)


def _nbr_bias_table(rpb):
    keys = NA_ROWS * GRID_W
    z = jnp.pad(rpb.astype(F32), ((0, 0), (0, 0), (0, GRID_W - (2 * NA_COLS - 1)))).reshape(B_HEADS, -1)
    z = jnp.stack([z[:, (NA_ROWS - 1 - v) * GRID_W:(NA_ROWS - 1 - v) * GRID_W + keys] for v in range(NA_ROWS)], axis=1)
    period = keys + GRID_W
    zz = jnp.concatenate([z[..., NA_COLS - 1:], jnp.zeros((B_HEADS, NA_ROWS, period - keys), F32), z[..., :NA_COLS - 1]], axis=-1)
    skew = jnp.tile(zz, (1, 1, GRID_W))[..., :GRID_W * (period - 1)].reshape(B_HEADS, NA_ROWS, GRID_W, period - 1)
    tab = skew[..., :keys]
    qc = jnp.arange(GRID_W)[:, None]
    kc = jnp.arange(keys)[None, :] % GRID_W
    cs = jnp.clip(qc - NA_COLS // 2, 0, GRID_W - NA_COLS)
    ok = (kc >= cs) & (kc < cs + NA_COLS)
    tab = jnp.where(ok[None, None], tab * LOG2E, NEG_INF)
    tab = tab.reshape(B_HEADS // 2, 2, NA_ROWS, GRID_W, keys)
    return jnp.transpose(tab, (0, 2, 1, 3, 4)).reshape(B_HEADS // 2, NA_ROWS, 2 * GRID_W, keys)


def _nbr_attn(qb, kb, vb, bias_tab):
    b, seq_len, _ = qb.shape
    n_rows = seq_len // GRID_W
    assert n_rows >= NA_ROWS and n_rows % NA_QROWS == 0
    qtile = NA_QROWS * GRID_W
    whole = pl.BlockSpec((None, seq_len, B_QKV), lambda s, m: (s, 0, 0))
    return pl.pallas_call(
        functools.partial(_nbr_attn_kernel, n_rows=n_rows),
        out_shape=jax.ShapeDtypeStruct((b, seq_len, B_QKV), BF16),
        grid=(b, n_rows // NA_QROWS),
        in_specs=[
            pl.BlockSpec((None, qtile, B_QKV), lambda s, m: (s, m, 0)),
            whole, whole,
            _const_spec((B_HEADS // 2, NA_ROWS, 2 * GRID_W, NA_ROWS * GRID_W)),
        ],
        out_specs=pl.BlockSpec((None, qtile, B_QKV), lambda s, m: (s, m, 0)),
        compiler_params=_params("parallel", "arbitrary"),
        name="nbr_attn",
    )(qb, kb, vb, bias_tab)


CONV_SHIFT = CONV_HALO - CONV_WIDTH // 2


def _conv_rows(r0, pad_ref, cw_ref, cb_ref, lng_ref, lnb_ref, yd_ref, stage_ref):
    span = CONV_ROWS + SUBLANES
    for lt in range(D_WIDTH // LANES):
        lanes = slice(lt * LANES, (lt + 1) * LANES)
        acc = None
        for b in range(SUBLANES):
            part = None
            for a in range(-(-(CONV_WIDTH + CONV_SHIFT) // SUBLANES)):
                k = SUBLANES * a + b - CONV_SHIFT
                if 0 <= k < CONV_WIDTH:
                    term = cw_ref[k:k + 1, lanes] * pad_ref[r0 + SUBLANES * a:r0 + SUBLANES * a + span, lanes]
                    part = term if part is None else part + term
            part = part[b:b + CONV_ROWS, :]
            acc = part if acc is None else acc + part
        stage_ref[:, lanes] = acc + cb_ref[:, lanes]
    conv = stage_ref[...]
    xc = conv - jnp.mean(conv, axis=-1, keepdims=True)
    y = xc * lax.rsqrt(jnp.mean(xc * xc, axis=-1, keepdims=True) + EPS) * lng_ref[...] + lnb_ref[...]
    yd_ref[r0:r0 + CONV_ROWS, :] = (y * jax.nn.sigmoid(y)).astype(BF16)


def _pre_ssm_kernel(x_ref, g1_ref, win_ref, wout_ref, g_ref, w_ref, x1_ref, uc_ref, adgd_ref, acc_ref):
    x1 = _ffn_body(x_ref[...], g1_ref, win_ref, wout_ref, acc_ref)
    x1_ref[...] = x1
    xn = _rms(x1, g_ref[...]).astype(BF16)
    uc_ref[...] = _dot(xn, w_ref[:, :C_WIDTH])
    adgd_ref[...] = _dot(xn, w_ref[:, C_WIDTH:])


def _pre_ssm(x, ffn, g, w):
    b, seq_len, _ = x.shape
    widths = (D_MODEL, C_WIDTH, 2 * D_WIDTH)
    return pl.pallas_call(
        _pre_ssm_kernel,
        out_shape=[jax.ShapeDtypeStruct((b, seq_len, n), F32) for n in widths],
        grid=(b, seq_len // TOKEN_TILE),
        in_specs=[_tok_spec(D_MODEL)] + _ffn_specs() + [
            _const_spec((1, D_MODEL)),
            _const_spec((D_MODEL, C_WIDTH + 2 * D_WIDTH)),
        ],
        out_specs=[_tok_spec(n) for n in widths],
        scratch_shapes=[pltpu.VMEM((TOKEN_TILE, D_MODEL), F32)],
        compiler_params=_params("parallel", "parallel"),
        name="pre_ssm",
    )(x, *ffn, g, w)


SLAB_GROUPS = LANES // SSM_GROUP_CH
N_SLABS = SSM_GROUPS // SLAB_GROUPS
SLAB_HALF = SLAB_GROUPS * SSM_STATE
SLAB = 2 * SLAB_HALF
STATE_W = N_SLABS * SLAB


def _ssm_scan_kernel(uf_ref, ub_ref, perm_ref, permt_ref, b_ref, a_ref, c_ref, yf_ref, yb_ref, s_ref, h_ref, yt_ref):
    c = pl.program_id(1)
    rows = SCAN_CHUNK * SCAN_SEQS

    @pl.when(c == 0)
    def _():
        h_ref[...] = jnp.zeros_like(h_ref)

    for d, u_ref in enumerate((uf_ref, ub_ref)):
        u2d = _dot(perm_ref[...], u_ref[...].reshape(rows, C_WIDTH).astype(BF16)).astype(BF16)
        for j in range(N_SLABS):
            bu = _dot(u2d[:, j * LANES:(j + 1) * LANES], b_ref[d, j])
            s_ref[d, :, :, j * SLAB:(j + 1) * SLAB] = bu.reshape(SCAN_CHUNK, SCAN_SEQS, SLAB)

    for pair in range(N_SLABS // 2):
        slabs = (2 * pair, 2 * pair + 1)

        def step(i, carry):
            new = []
            for d in range(2):
                t = i if d == 0 else SCAN_CHUNK - 1 - i
                for idx, j in enumerate(slabs):
                    hr, hi = carry[2 * (2 * d + idx)], carry[2 * (2 * d + idx) + 1]
                    re = slice(j * SLAB, j * SLAB + SLAB_HALF)
                    im = slice(j * SLAB + SLAB_HALF, (j + 1) * SLAB)
                    ar = a_ref[d, :, re]
                    ai = a_ref[d, :, im]
                    nr = ar * hr - ai * hi + s_ref[d, t, :, re]
                    ni = ar * hi + ai * hr + s_ref[d, t, :, im]
                    s_ref[d, t, :, re] = nr
                    s_ref[d, t, :, im] = ni
                    new += [nr, ni]
            return tuple(new)

        init = []
        for d in range(2):
            for j in slabs:
                init += [h_ref[d, :, j * SLAB:j * SLAB + SLAB_HALF],
                         h_ref[d, :, j * SLAB + SLAB_HALF:(j + 1) * SLAB]]
        final = lax.fori_loop(0, SCAN_CHUNK, step, tuple(init), unroll=True)
        k = 0
        for d in range(2):
            for j in slabs:
                h_ref[d, :, j * SLAB:j * SLAB + SLAB_HALF] = final[k]
                h_ref[d, :, j * SLAB + SLAB_HALF:(j + 1) * SLAB] = final[k + 1]
                k += 2

    for d, y_ref in enumerate((yf_ref, yb_ref)):
        for j in range(N_SLABS):
            hs = s_ref[d, :, :, j * SLAB:(j + 1) * SLAB].reshape(rows, SLAB).astype(BF16)
            yt_ref[:, j * LANES:(j + 1) * LANES] = _dot(hs, c_ref[d, j])
        y_sb = _dot(permt_ref[...], yt_ref[...].astype(BF16))
        y_ref[...] = y_sb.astype(BF16).reshape(SCAN_SEQS, SCAN_CHUNK, C_WIDTH)


def _ssm_scan_weights(lam_re, lam_im, log_dt, b_re, b_im, c_re, c_im):
    lam = lax.complex(lam_re.astype(F32), lam_im.astype(F32))
    dt = jnp.exp(log_dt.astype(F32))[:, :, None]
    lam_bar = jnp.exp(lam * dt)
    b_bar = ((lam_bar - 1.0) / lam)[..., None] * lax.complex(b_re.astype(F32), b_im.astype(F32))
    eye = jnp.eye(SLAB_GROUPS, dtype=F32)

    def pack_b(part):
        x = part.reshape(2, N_SLABS, SLAB_GROUPS, SSM_STATE, SSM_GROUP_CH)
        x = jnp.einsum('dsgnp,gh->dsgphn', x, eye)
        return x.reshape(2, N_SLABS, LANES, SLAB_HALF)

    b_pack = jnp.concatenate([pack_b(jnp.real(b_bar)), pack_b(jnp.imag(b_bar))], axis=-1).astype(BF16)

    def pack_a(part):
        return part.reshape(2, N_SLABS, SLAB_HALF)

    a_pack = jnp.concatenate([pack_a(jnp.real(lam_bar)), pack_a(jnp.imag(lam_bar))], axis=-1)
    a_pack = jnp.broadcast_to(a_pack.reshape(2, 1, STATE_W), (2, SCAN_SEQS, STATE_W))

    def pack_c(part):
        x = part.astype(F32).reshape(2, N_SLABS, SLAB_GROUPS, SSM_GROUP_CH, SSM_STATE)
        x = jnp.einsum('dsgpn,gh->dsgnhp', x, eye)
        return x.reshape(2, N_SLABS, SLAB_HALF, LANES)

    c_pack = jnp.concatenate([pack_c(c_re), -pack_c(c_im)], axis=2).astype(BF16)
    return b_pack, a_pack, c_pack


def _ssm_scan(u, b_pack, a_pack, c_pack):
    b, seq_len, _ = u.shape
    nc = seq_len // SCAN_CHUNK
    rows = SCAN_CHUNK * SCAN_SEQS
    blk = (SCAN_SEQS, SCAN_CHUNK, C_WIDTH)
    r = jnp.arange(rows)
    perm = (r[:, None] == (r[None, :] % SCAN_CHUNK) * SCAN_SEQS + r[None, :] // SCAN_CHUNK).astype(BF16)
    return pl.pallas_call(
        _ssm_scan_kernel,
        out_shape=[jax.ShapeDtypeStruct(u.shape, BF16)] * 2,
        grid=(b // SCAN_SEQS, nc),
        in_specs=[
            pl.BlockSpec(blk, lambda s, c: (s, c, 0)),
            pl.BlockSpec(blk, lambda s, c: (s, nc - 1 - c, 0)),
            _const_spec((rows, rows)),
            _const_spec((rows, rows)),
            _const_spec((2, N_SLABS, LANES, SLAB)),
            _const_spec((2, SCAN_SEQS, STATE_W)),
            _const_spec((2, N_SLABS, SLAB, LANES)),
        ],
        out_specs=[
            pl.BlockSpec(blk, lambda s, c: (s, c, 0)),
            pl.BlockSpec(blk, lambda s, c: (s, nc - 1 - c, 0)),
        ],
        scratch_shapes=[pltpu.VMEM((2, SCAN_CHUNK, SCAN_SEQS, STATE_W), F32),
                        pltpu.VMEM((2, SCAN_SEQS, STATE_W), F32),
                        pltpu.VMEM((rows, C_WIDTH), F32)],
        compiler_params=_params("parallel", "arbitrary"),
        name="ssm_scan",
    )(u, u, perm, perm.T, b_pack, a_pack, c_pack)


def _ssm_out_kernel(x_ref, yf_ref, yb_ref, uc_ref, cur_ref, prev_ref, next_ref,
                    dskip_ref, wglu_ref, bglu_ref, cw_ref, cb_ref, lng_ref, lnb_ref, w_ref,
                    o_ref, pad_ref, yd_ref, stage_ref):
    i = pl.program_id(1)
    last = pl.num_programs(1) - 1
    tile = cur_ref.shape[0]

    z = _gelu_tanh(yf_ref[...].astype(F32) + yb_ref[...].astype(F32) + dskip_ref[...] * uc_ref[...])
    yc = z * jax.nn.sigmoid(_dot(z.astype(BF16), wglu_ref[...]) + bglu_ref[...])

    def glu(ref):
        v = ref[...]
        return v[:, :D_WIDTH] * jax.nn.sigmoid(v[:, D_WIDTH:])

    pad_ref[0:CONV_HALO, :] = jnp.where(i > 0, glu(prev_ref), 0.0)
    pad_ref[CONV_HALO:CONV_HALO + tile, :] = glu(cur_ref)
    pad_ref[CONV_HALO + tile:, :] = jnp.where(i < last, glu(next_ref), 0.0)
    for r0 in range(0, tile, CONV_ROWS):
        _conv_rows(r0, pad_ref, cw_ref, cb_ref, lng_ref, lnb_ref, yd_ref, stage_ref)

    o_ref[...] = (x_ref[...]
                  + _dot(yc.astype(BF16), w_ref[:C_WIDTH, :])
                  + _dot(yd_ref[...], w_ref[C_WIDTH:, :]))


def _ssm_out(x, yf, yb, uc, adgd, d_skip, w_glu, b_glu, conv_w, conv_b, ln_g, ln_b, w_out):
    b, seq_len, _ = x.shape
    tile = min(SSM_OUT_TILE, seq_len)
    assert seq_len % tile == 0
    per = tile // CONV_HALO
    n_halo = seq_len // CONV_HALO

    def tok(width):
        return pl.BlockSpec((None, tile, width), lambda s, i: (s, i, 0))

    return pl.pallas_call(
        _ssm_out_kernel,
        out_shape=jax.ShapeDtypeStruct(x.shape, F32),
        grid=(b, seq_len // tile),
        in_specs=[
            tok(D_MODEL), tok(C_WIDTH), tok(C_WIDTH), tok(C_WIDTH), tok(2 * D_WIDTH),
            pl.BlockSpec((None, CONV_HALO, 2 * D_WIDTH), lambda s, i: (s, jnp.maximum(i * per - 1, 0), 0)),
            pl.BlockSpec((None, CONV_HALO, 2 * D_WIDTH),
                         lambda s, i: (s, jnp.minimum((i + 1) * per, n_halo - 1), 0)),
            _const_spec((1, C_WIDTH)),
            _const_spec((C_WIDTH, C_WIDTH)),
            _const_spec((1, C_WIDTH)),
            _const_spec((CONV_WIDTH, D_WIDTH)),
            _const_spec((1, D_WIDTH)),
            _const_spec((1, D_WIDTH)),
            _const_spec((1, D_WIDTH)),
            _const_spec((C_WIDTH + D_WIDTH, D_MODEL)),
        ],
        out_specs=tok(D_MODEL),
        scratch_shapes=[pltpu.VMEM((tile + 2 * CONV_HALO, D_WIDTH), F32),
                        pltpu.VMEM((tile, D_WIDTH), BF16),
                        pltpu.VMEM((CONV_ROWS, D_WIDTH), F32)],
        compiler_params=_params("parallel", "arbitrary"),
        name="ssm_out",
    )(x, yf, yb, uc, adgd, adgd, adgd, d_skip, w_glu, b_glu, conv_w, conv_b, ln_g, ln_b, w_out)


def _row(v):
    return v.astype(F32).reshape(1, -1)


def _attn_layer(x, ffn1, g_mix, w, j):
    w_in, gain = w['attn_in'][j]
    x1, qa, kva, qb, kb, vb = _pre_attn(x, ffn1, g_mix, w_in, gain)
    ya = _win_gqa(qa, kva, w['sink_a'][j].astype(F32) * LOG2E, w['win_bias'])
    yb = _nbr_attn(qb, kb, vb, w['nbr_bias'][j])
    return x1, (ya, yb, w['w_attn_out'][j])


def _ssm_layer(x, ffn1, g_mix, w, j):
    x1, uc, adgd = _pre_ssm(x, ffn1, g_mix, w['w_ssm_in'][j])
    b_pack, a_pack, c_pack = w['scan'][j]
    yf, yb = _ssm_scan(uc, b_pack, a_pack, c_pack)
    x2 = _ssm_out(x1, yf, yb, uc, adgd,
                  _row(w['d_skip'][j]), w['w_glu_c'][j], _row(w['b_glu_c'][j]),
                  w['conv_w'][j].astype(F32), _row(w['conv_b'][j]), _row(w['ln_g_d'][j]),
                  _row(w['ln_b_d'][j]), w['w_ssm_out'][j])
    return x2, ()


def _trunk(x, p, w):
    b, seq_len, _ = x.shape
    assert seq_len % TOKEN_TILE == 0 and seq_len % SCAN_CHUNK == 0 and b % SCAN_SEQS == 0
    assert seq_len >= WIN_SPAN and seq_len % (WIN * WIN_QBLOCKS) == 0
    for i in range(DEPTH):
        ffn1 = (_row(w['norm_ffn1'][i]), w['w_ffn1_in'][i], w['w_ffn1_out'][i])
        ffn2 = (_row(w['norm_ffn2'][i]), w['w_ffn2_in'][i], w['w_ffn2_out'][i])
        ple = (_row(w['norm_ple'][i]), w['w_ple_gate'][i], w['w_ple_proj'][i], _row(w['norm_ple_post'][i]))
        layer = _attn_layer if i % 2 == 0 else _ssm_layer
        x, mix = layer(x, ffn1, _row(w['norm_mix'][i]), w, i // 2)
        x = _post(x, mix, ffn2, p, i, ple)
    return x


def kernel(x_prompt, x_sample, p_prompt, p_sample, norm_ffn1, w_ffn1_in, w_ffn1_out, norm_mix, norm_ffn2, w_ffn2_in, w_ffn2_out, norm_ple, w_ple_gate, w_ple_proj, norm_ple_post, w_attn_in, q_gain_a, k_gain_a, sink_a, q_gain_b, k_gain_b, rpb_b, w_attn_out, w_ssm_in, lam_re, lam_im, log_dt, b_re, b_im, c_re, c_im, d_skip, w_glu_c, b_glu_c, conv_w, conv_b, ln_g_d, ln_b_d, w_ssm_out):
    w = dict(norm_ffn1=norm_ffn1, norm_mix=norm_mix, norm_ffn2=norm_ffn2, norm_ple=norm_ple,
             norm_ple_post=norm_ple_post, q_gain_a=q_gain_a, k_gain_a=k_gain_a, sink_a=sink_a,
             q_gain_b=q_gain_b, k_gain_b=k_gain_b, d_skip=d_skip, b_glu_c=b_glu_c, conv_w=conv_w,
             conv_b=conv_b, ln_g_d=ln_g_d, ln_b_d=ln_b_d)
    for name, val in dict(w_ffn1_in=w_ffn1_in, w_ffn1_out=w_ffn1_out, w_ffn2_in=w_ffn2_in,
                          w_ffn2_out=w_ffn2_out, w_ple_gate=w_ple_gate, w_ple_proj=w_ple_proj,
                          w_attn_out=w_attn_out, w_ssm_in=w_ssm_in,
                          w_glu_c=w_glu_c, w_ssm_out=w_ssm_out).items():
        w[name] = val.astype(BF16)
    w['nbr_bias'] = [_nbr_bias_table(rpb_b[j]) for j in range(rpb_b.shape[0])]
    w['win_bias'] = _win_bias_table()
    w['attn_in'] = [_attn_in_weights(w_attn_in[j], q_gain_a[j], k_gain_a[j], q_gain_b[j], k_gain_b[j])
                    for j in range(w_attn_in.shape[0])]
    w['scan'] = [_ssm_scan_weights(lam_re[j], lam_im[j], log_dt[j], b_re[j], b_im[j], c_re[j], c_im[j])
                 for j in range(lam_re.shape[0])]
    return (_trunk(x_prompt, p_prompt, w), _trunk(x_sample, p_sample, w))
```
